```python
import math
import jax, jax.numpy as jnp
from jax import lax
import numpy as np

D_MODEL = 4096
BATCH = 4
SEQ = 2048
DEPTH = 4
DEC_BATCH = 8
DEC_SEQ = 1
PAST_LEN = 8192
PAGE_SIZE = 128

N_MIXERS = 4
N_A = len(range(0, DEPTH, N_MIXERS))
N_B = len(range(1, DEPTH, N_MIXERS))
N_C = len(range(2, DEPTH, N_MIXERS))
N_D = len(range(3, DEPTH, N_MIXERS))
ALPHA = (2.0 * DEPTH) ** 0.25
BETA_INIT = (8.0 * DEPTH) ** -0.25
CONV_W = 4
LN_EPS = 1e-5
NORM_EPS = 1e-6
H_A = 32
DK_A = 128
DV_A = 128
D_QK_A = H_A * DK_A
D_V_A = H_A * DV_A
GDN_CONV_DIM = 2 * D_QK_A + D_V_A
GDN_CHUNK = 64
D_RNN = (D_MODEL * 5) // 4
NB_B = 16
BW_B = D_RNN // NB_B
LRU_C = 8.0
H_C = 16
DK_C = D_MODEL // H_C
DV_C = 2 * DK_C
D_QK_C = H_C * DK_C
D_V_C = H_C * DV_C
RET_CHUNK = 64
H_D = 32
HD_D = D_MODEL // H_D
MOBA_BLOCK = 256
MOBA_TOPK = 3
MOBA_QB = 4
P_HEADS = 8
P_DKEY = 256
N_KEYS = 128
N_EXPERTS = N_KEYS * N_KEYS
P_TOPK = 16
PEER_TOKEN_CHUNK = 128

F32 = jnp.float32

kernel_name = 'hybrid_gdn_rglru_retnet_moba_peer_step'


def layer_norm(x, g, b):
    xf = x.astype(F32)
    mu = jnp.mean(xf, -1, keepdims=True)
    var = jnp.mean(jnp.square(xf - mu), -1, keepdims=True)
    return ((xf - mu) * lax.rsqrt(var + LN_EPS) * g.astype(F32) + b.astype(F32)).astype(x.dtype)


def l2norm(x):
    return x * lax.rsqrt(jnp.sum(x * x, -1, keepdims=True) + NORM_EPS)


def alibi_slopes():
    return jnp.exp2(-8.0 * (jnp.arange(H_D, dtype=F32) + 1.0) / H_D)


def retention_log_gamma():
    return jnp.log1p(-jnp.exp2(-5.0 - jnp.arange(H_C, dtype=F32)))


def causal_dwconv(x, buf, w, b=None):
    L = x.shape[1]
    xp = jnp.concatenate([buf.astype(x.dtype), x], axis=1)
    y = xp[:, 0:L] * w[0]
    for j in range(1, CONV_W):
        y = y + xp[:, j:j + L] * w[j]
    if b is not None:
        y = y + b
    return y, xp[:, L:]


def to_chunks(t, c):
    B, L, H = t.shape[:3]
    t = t.reshape((B, L // c, c, H) + t.shape[3:])
    return jnp.moveaxis(t, (1, 3), (0, 2))


def from_chunks(t):
    n, B, H, c = t.shape[:4]
    return jnp.moveaxis(t, (0, 2), (1, 3)).reshape((B, n * c, H) + t.shape[4:])


def gated_delta_rule(q, k, v, g, beta, S0):
    L = q.shape[1]
    c = math.gcd(L, GDN_CHUNK)
    qc, kc, vc = to_chunks(q, c), to_chunks(k, c), to_chunks(v, c)
    gc = jnp.cumsum(to_chunks(g, c), axis=-1)
    bc = to_chunks(beta, c)[..., None]
    tri = jnp.tril(jnp.ones((c, c), dtype=bool))
    strict = jnp.tril(jnp.ones((c, c), dtype=bool), -1)
    diff = gc[..., :, None] - gc[..., None, :]
    decay = jnp.where(tri, jnp.exp(jnp.where(tri, diff, 0.0)), 0.0)
    kb = kc * bc
    m = jnp.where(strict, jnp.einsum('nbhik,nbhjk->nbhij', kb, kc) * decay, 0.0)
    a = jnp.eye(c, dtype=m.dtype) + m
    u = lax.linalg.triangular_solve(a, vc * bc, left_side=True, lower=True)
    w = lax.linalg.triangular_solve(a, kb * jnp.exp(gc)[..., None], left_side=True, lower=True)
    qk = jnp.where(tri, jnp.einsum('nbhik,nbhjk->nbhij', qc, kc) * decay, 0.0)

    def step(S, xs):
        qi, ki, ui, wi, qki, gi = xs
        v_new = ui - jnp.einsum('bhck,bhkv->bhcv', wi, S)
        o = (jnp.einsum('bhck,bhkv->bhcv', qi * jnp.exp(gi)[..., None], S)
             + jnp.einsum('bhij,bhjv->bhiv', qki, v_new))
        g_last = gi[..., -1:]
        S = (S * jnp.exp(g_last)[..., None]
             + jnp.einsum('bhck,bhcv->bhkv', ki * jnp.exp(g_last - gi)[..., None], v_new))
        return S, o

    S, o = lax.scan(step, S0, (qc, kc, u, w, qk, gc))
    return from_chunks(o), S


def retention_chunkwise(q, k, v, log_gamma, S0):
    L = q.shape[1]
    c = math.gcd(L, RET_CHUNK)
    idx = jnp.arange(c, dtype=F32)
    diff = idx[:, None] - idx[None, :]
    tri = diff >= 0
    decay = jnp.where(tri, jnp.exp(jnp.where(tri, diff, 0.0) * log_gamma[:, None, None]), 0.0)
    q_dec = jnp.exp((idx + 1.0) * log_gamma[:, None])[..., None]
    k_dec = jnp.exp((c - 1.0 - idx) * log_gamma[:, None])[..., None]
    blk_dec = jnp.exp(c * log_gamma)[:, None, None]

    def step(S, xs):
        qi, ki, vi = xs
        s = jnp.einsum('bhik,bhjk->bhij', qi, ki) * decay
        o = jnp.einsum('bhij,bhjv->bhiv', s, vi) + jnp.einsum('bhik,bhkv->bhiv', qi * q_dec, S)
        S = blk_dec * S + jnp.einsum('bhjk,bhjv->bhkv', ki * k_dec, vi)
        return S, o

    S, o = lax.scan(step, S0, (to_chunks(q, c), to_chunks(k, c), to_chunks(v, c)))
    return from_chunks(o), S


def lru_combine(e1, e2):
    a1, b1 = e1
    a2, b2 = e2
    return a1 * a2, a2 * b1 + b2


def moba_attend(q, k_parts, v_parts):
    B, Lq, H, hd = q.shape
    Lk = sum(p.shape[1] for p in k_parts)
    offset = Lk - Lq
    n_blk = -(-Lk // MOBA_BLOCK)
    pad = jnp.zeros((B, n_blk * MOBA_BLOCK - Lk, H, hd), q.dtype)
    kb = jnp.concatenate(list(k_parts) + [pad.astype(k_parts[0].dtype)], axis=1).reshape(B, n_blk, MOBA_BLOCK, H, hd)
    vb = jnp.concatenate(list(v_parts) + [pad.astype(v_parts[0].dtype)], axis=1).reshape(B, n_blk, MOBA_BLOCK, H, hd)
    k_mean = jnp.mean(kb.astype(F32), axis=2)
    q_pos = offset + jnp.arange(Lq)
    q_blk = q_pos // MOBA_BLOCK
    gate = jnp.einsum('bqhd,bnhd->bqhn', q.astype(F32), k_mean)
    past = jnp.arange(n_blk)[None, :] < q_blk[:, None]
    gate = jnp.where(past[None, :, None, :], gate, -jnp.inf)
    k_sel = min(MOBA_TOPK, n_blk)
    top_s, top_i = lax.top_k(gate, k_sel)
    own = jnp.broadcast_to(q_blk[None, :, None, None], (B, Lq, H, 1)).astype(top_i.dtype)
    blk_idx = jnp.concatenate([top_i, own], axis=-1)
    blk_ok = jnp.concatenate([top_s > -jnp.inf, jnp.ones((B, Lq, H, 1), bool)], axis=-1)
    slopes = alibi_slopes()[:, None, None]
    bi = jnp.arange(B)[:, None, None, None]
    hi = jnp.arange(H)[None, None, :, None]
    qb = math.gcd(Lq, MOBA_QB)
    nq = Lq // qb

    def attend(args):
        qc, ic, okc, pc = args
        kg = kb[bi, ic, :, hi, :]
        vg = vb[bi, ic, :, hi, :]
        s = jnp.einsum('bqhd,bqhsjd->bqhsj', qc, kg).astype(F32)
        key_pos = ic[..., None] * MOBA_BLOCK + jnp.arange(MOBA_BLOCK)
        dist = (pc[None, :, None, None, None] - key_pos).astype(F32)
        s = jnp.where(okc[..., None] & (dist >= 0), s - slopes * dist, -jnp.inf)
        p = jax.nn.softmax(s.reshape(s.shape[:3] + (-1,)), axis=-1).reshape(s.shape)
        return jnp.einsum('bqhsj,bqhsjd->bqhd', p.astype(vg.dtype), vg)

    def qchunks(t):
        return jnp.moveaxis(t.reshape((B, nq, qb) + t.shape[2:]), 1, 0)

    o = lax.map(attend, (qchunks(q), qchunks(blk_idx), qchunks(blk_ok), q_pos.reshape(nq, qb)))
    return jnp.moveaxis(o, 0, 1).reshape(B, Lq, H, hd)


def gdn_mixer(h, state, params):
    S0, conv0 = state
    w_in, conv_w, a_log, dt_bias, norm_w, w_out = params
    B, L, _ = h.shape
    proj = h @ w_in
    qkv, z, b_raw, a_raw = jnp.split(proj, [GDN_CONV_DIM, GDN_CONV_DIM + D_V_A, GDN_CONV_DIM + D_V_A + H_A], axis=-1)
    qkv, conv_new = causal_dwconv(qkv, conv0, conv_w)
    qkv = jax.nn.silu(qkv.astype(F32))
    q, k, v = jnp.split(qkv, [D_QK_A, 2 * D_QK_A], axis=-1)
    q = l2norm(q.reshape(B, L, H_A, DK_A)) * (DK_A ** -0.5)
    k = l2norm(k.reshape(B, L, H_A, DK_A))
    v = v.reshape(B, L, H_A, DV_A)
    beta = jax.nn.sigmoid(b_raw.astype(F32))
    g = -jnp.exp(a_log.astype(F32)) * jax.nn.softplus(a_raw.astype(F32) + dt_bias.astype(F32))
    o, S = gated_delta_rule(q, k, v, g, beta, S0.astype(F32))
    o = o * lax.rsqrt(jnp.mean(jnp.square(o), -1, keepdims=True) + NORM_EPS) * norm_w.astype(F32)
    o = o * jax.nn.silu(z.astype(F32).reshape(B, L, H_A, DV_A))
    out = o.reshape(B, L, D_V_A).astype(h.dtype) @ w_out
    return out, (S.astype(S0.dtype), conv_new.astype(conv0.dtype))


def lru_mixer(h, state, params):
    h0, conv0 = state
    w_in, conv_w, conv_b, w_gates, b_gates, lam, w_out = params
    B, L, _ = h.shape
    gate_in, rec_in = jnp.split(h @ w_in, 2, axis=-1)
    xc, conv_new = causal_dwconv(rec_in, conv0, conv_w, conv_b)
    xf = xc.astype(F32)
    gl = (jnp.einsum('blni,gnij->gblnj', xf.reshape(B, L, NB_B, BW_B), w_gates.astype(F32)).reshape(2, B, L, D_RNN)
          + b_gates.astype(F32)[:, None, None, :])
    r, i_gate = jax.nn.sigmoid(gl[0]), jax.nn.sigmoid(gl[1])
    log_a = -LRU_C * r * jax.nn.softplus(-lam.astype(F32))
    a = jnp.exp(log_a)
    b = jnp.sqrt(-jnp.expm1(2.0 * log_a)) * (i_gate * xf)
    b = b.at[:, 0].add(a[:, 0] * h0.astype(F32))
    _, hs = lax.associative_scan(lru_combine, (a, b), axis=1)
    y = hs * jax.nn.gelu(gate_in.astype(F32))
    out = y.astype(h.dtype) @ w_out
    return out, (hs[:, -1].astype(h0.dtype), conv_new.astype(conv0.dtype))


def ret_mixer(h, state, params):
    (S0,) = state
    w_in, gn_w, w_out = params
    B, L, _ = h.shape
    q, k, v, gz = jnp.split(h @ w_in, [D_QK_C, 2 * D_QK_C, 2 * D_QK_C + D_V_C], axis=-1)
    q = q.astype(F32).reshape(B, L, H_C, DK_C)
    k = k.astype(F32).reshape(B, L, H_C, DK_C) * (DK_C ** -0.5)
    v = v.astype(F32).reshape(B, L, H_C, DV_C)
    o, S = retention_chunkwise(q, k, v, retention_log_gamma(), S0.astype(F32))
    mu = jnp.mean(o, -1, keepdims=True)
    var = jnp.mean(jnp.square(o - mu), -1, keepdims=True)
    o = (o - mu) * lax.rsqrt(var + NORM_EPS) * gn_w.astype(F32)
    y = jax.nn.silu(gz.astype(F32)) * o.reshape(B, L, D_V_C)
    out = y.astype(h.dtype) @ w_out
    return out, (S.astype(S0.dtype),)


def moba_mixer(h, state, params):
    w_in, w_out = params
    B, L, _ = h.shape
    q, k, v = jnp.split(h @ w_in, 3, axis=-1)
    q = q.reshape(B, L, H_D, HD_D) * (HD_D ** -0.5)
    k = k.reshape(B, L, H_D, HD_D)
    v = v.reshape(B, L, H_D, HD_D)
    k_parts = [state[0], k] if state else [k]
    v_parts = [state[1], v] if state else [v]
    o = moba_attend(q, k_parts, v_parts)
    out = o.reshape(B, L, D_MODEL).astype(h.dtype) @ w_out
    return out, (k, v)


def peer_ffn(h, wq, keys, u, v):
    B, L, D = h.shape
    T = B * L
    t = h.reshape(T, D)
    q = (t @ wq).astype(F32).reshape(T, P_HEADS, 2, P_DKEY // 2)
    s = jnp.einsum('thpd,hpkd->thpk', q, keys.astype(F32))
    s_top, i_top = lax.top_k(s, P_TOPK)
    cand_s = (s_top[:, :, 0, :, None] + s_top[:, :, 1, None, :]).reshape(T, P_HEADS, P_TOPK * P_TOPK)
    cand_e = (i_top[:, :, 0, :, None] * N_KEYS + i_top[:, :, 1, None, :]).reshape(T, P_HEADS, P_TOPK * P_TOPK)
    best_s, best_pos = lax.top_k(cand_s, P_TOPK)
    experts = jnp.take_along_axis(cand_e, best_pos, axis=-1)
    gates = jax.nn.softmax(best_s, axis=-1)
    tc = math.gcd(T, PEER_TOKEN_CHUNK)
    n = T // tc

    def chunk(args):
        xc, ec, gc = args
        act = jax.nn.gelu(jnp.einsum('td,thkd->thk', xc, u[ec]).astype(F32))
        return jnp.einsum('thk,thkd->td', (gc * act).astype(v.dtype), v[ec])

    out = lax.map(chunk, (t.reshape(n, tc, D), experts.reshape(n, tc, P_HEADS, P_TOPK),
                          gates.reshape(n, tc, P_HEADS, P_TOPK)))
    return out.reshape(B, L, D).astype(h.dtype)


def trunk_layer(x, c_act, w_ada, b_ada, ln_g, ln_b, peer_params, mixer, mixer_params, state):
    mod = (c_act @ w_ada + b_ada)[:, None, :]
    sh1, sc1, g1, sh2, sc2, g2 = jnp.split(mod, 6, axis=-1)
    y, new_state = mixer(x * (1 + sc1) + sh1, state, mixer_params)
    x = layer_norm(ALPHA * x + (1 + g1) * y, ln_g[0], ln_b[0])
    y = peer_ffn(x * (1 + sc2) + sh2, *peer_params)
    x = layer_norm(ALPHA * x + (1 + g2) * y, ln_g[1], ln_b[1])
    return x, new_state


def setup_inputs(seed: int = 0) -> dict:
    key = jax.random.key(seed)
    keys = iter(jax.random.split(key, 48))

    def nrm(shape, scale):
        return jax.random.normal(next(keys), shape, F32) * scale

    def uni(shape, lo, hi):
        return jax.random.uniform(next(keys), shape, F32, lo, hi)

    n_pages = PAST_LEN // PAGE_SIZE
    n_used = DEC_BATCH * n_pages
    n_pool = n_used + (n_used + 3) // 4
    x_prompt = nrm((BATCH, SEQ, D_MODEL), 1.0)
    x_sample = nrm((DEC_BATCH, DEC_SEQ, D_MODEL), 1.0)
    c_prompt = nrm((BATCH, D_MODEL), 1.0)
    c_sample = nrm((DEC_BATCH, D_MODEL), 1.0)
    state_gdn_S = nrm((N_A, DEC_BATCH, H_A, DK_A, DV_A), 0.1)
    state_gdn_conv = nrm((N_A, DEC_BATCH, CONV_W - 1, GDN_CONV_DIM), 1.0)
    state_lru_h = nrm((N_B, DEC_BATCH, D_RNN), 0.5)
    state_lru_conv = nrm((N_B, DEC_BATCH, CONV_W - 1, D_RNN), 1.0)
    state_ret_S = nrm((N_C, DEC_BATCH, H_C, DK_C, DV_C), 1.0)
    cache_k = nrm((N_D, n_pool, PAGE_SIZE, H_D, HD_D), 1.0)
    cache_v = nrm((N_D, n_pool, PAGE_SIZE, H_D, HD_D), 1.0)
    page_table = jax.random.permutation(next(keys), n_pool)[:n_used].reshape(DEC_BATCH, n_pages).astype(jnp.int32)
    w_ada = nrm((DEPTH, D_MODEL, 6 * D_MODEL), 0.2 * D_MODEL ** -0.5)
    b_ada = nrm((DEPTH, 6 * D_MODEL), 0.01)
    ln_g = 1.0 + nrm((DEPTH, 2, D_MODEL), 0.02)
    ln_b = nrm((DEPTH, 2, D_MODEL), 0.02)
    peer_wq = nrm((DEPTH, D_MODEL, P_HEADS * P_DKEY), D_MODEL ** -0.5)
    peer_keys = nrm((DEPTH, P_HEADS, 2, N_KEYS, P_DKEY // 2), (P_DKEY // 2) ** -0.5)
    peer_u = nrm((DEPTH, N_EXPERTS, D_MODEL), D_MODEL ** -0.5)
    peer_v = nrm((DEPTH, N_EXPERTS, D_MODEL), 0.5 * BETA_INIT)
    gdn_w_in = nrm((N_A, D_MODEL, GDN_CONV_DIM + D_V_A + 2 * H_A), D_MODEL ** -0.5)
    gdn_conv_w = nrm((N_A, CONV_W, GDN_CONV_DIM), 0.5)
    gdn_a_log = jnp.log(uni((N_A, H_A), 1.0, 16.0))
    dt = jnp.exp(uni((N_A, H_A), math.log(1e-3), math.log(1e-1)))
    gdn_dt_bias = dt + jnp.log(-jnp.expm1(-dt))
    gdn_norm_w = 1.0 + nrm((N_A, DV_A), 0.02)
    gdn_w_out = nrm((N_A, D_V_A, D_MODEL), BETA_INIT * D_V_A ** -0.5)
    lru_w_in = nrm((N_B, D_MODEL, 2 * D_RNN), D_MODEL ** -0.5)
    lru_conv_w = nrm((N_B, CONV_W, D_RNN), 0.5)
    lru_conv_b = nrm((N_B, D_RNN), 0.01)
    lru_w_gates = nrm((N_B, 2, NB_B, BW_B, BW_B), BW_B ** -0.5)
    lru_b_gates = nrm((N_B, 2, D_RNN), 0.01)
    a_base = uni((N_B, D_RNN), 0.9, 0.999) ** (1.0 / LRU_C)
    lru_lambda = jnp.log(a_base) - jnp.log1p(-a_base)
    lru_w_out = nrm((N_B, D_RNN, D_MODEL), BETA_INIT * D_RNN ** -0.5)
    ret_w_in = nrm((N_C, D_MODEL, 2 * D_QK_C + 2 * D_V_C), D_MODEL ** -0.5)
    ret_gn_w = 1.0 + nrm((N_C, H_C, DV_C), 0.02)
    ret_w_out = nrm((N_C, D_V_C, D_MODEL), BETA_INIT * D_V_C ** -0.5)
    moba_w_in = nrm((N_D, D_MODEL, 3 * D_MODEL), D_MODEL ** -0.5)
    moba_w_out = nrm((N_D, D_MODEL, D_MODEL), BETA_INIT * D_MODEL ** -0.5)
    return {
        'x_prompt': x_prompt, 'x_sample': x_sample, 'c_prompt': c_prompt, 'c_sample': c_sample,
        'state_gdn_S': state_gdn_S, 'state_gdn_conv': state_gdn_conv,
        'state_lru_h': state_lru_h, 'state_lru_conv': state_lru_conv, 'state_ret_S': state_ret_S,
        'cache_k': cache_k, 'cache_v': cache_v, 'page_table': page_table,
        'w_ada': w_ada, 'b_ada': b_ada, 'ln_g': ln_g, 'ln_b': ln_b,
        'peer_wq': peer_wq, 'peer_keys': peer_keys, 'peer_u': peer_u, 'peer_v': peer_v,
        'gdn_w_in': gdn_w_in, 'gdn_conv_w': gdn_conv_w, 'gdn_a_log': gdn_a_log, 'gdn_dt_bias': gdn_dt_bias,
        'gdn_norm_w': gdn_norm_w, 'gdn_w_out': gdn_w_out,
        'lru_w_in': lru_w_in, 'lru_conv_w': lru_conv_w, 'lru_conv_b': lru_conv_b, 'lru_w_gates': lru_w_gates,
        'lru_b_gates': lru_b_gates, 'lru_lambda': lru_lambda, 'lru_w_out': lru_w_out,
        'ret_w_in': ret_w_in, 'ret_gn_w': ret_gn_w, 'ret_w_out': ret_w_out,
        'moba_w_in': moba_w_in, 'moba_w_out': moba_w_out,
    }


def reference(x_prompt, x_sample, c_prompt, c_sample, state_gdn_S, state_gdn_conv, state_lru_h, state_lru_conv,
              state_ret_S, cache_k, cache_v, page_table, w_ada, b_ada, ln_g, ln_b, peer_wq, peer_keys, peer_u, peer_v,
              gdn_w_in, gdn_conv_w, gdn_a_log, gdn_dt_bias, gdn_norm_w, gdn_w_out,
              lru_w_in, lru_conv_w, lru_conv_b, lru_w_gates, lru_b_gates, lru_lambda, lru_w_out,
              ret_w_in, ret_gn_w, ret_w_out, moba_w_in, moba_w_out):
    n_seq, n_pages = page_table.shape
    past_len = n_pages * cache_k.shape[2]
    c_act = (jax.nn.silu(c_prompt), jax.nn.silu(c_sample))
    xs = [x_prompt, x_sample]
    new_states = {(m, grp): [] for m in range(N_MIXERS) for grp in range(2)}
    for i in range(DEPTH):
        m, j = i % N_MIXERS, i // N_MIXERS
        if m == 0:
            mixer = gdn_mixer
            params = (gdn_w_in[j], gdn_conv_w[j], gdn_a_log[j], gdn_dt_bias[j], gdn_norm_w[j], gdn_w_out[j])
            states = ((jnp.zeros((BATCH, H_A, DK_A, DV_A), state_gdn_S.dtype),
                       jnp.zeros((BATCH, CONV_W - 1, GDN_CONV_DIM), state_gdn_conv.dtype)),
                      (state_gdn_S[j], state_gdn_conv[j]))
        elif m == 1:
            mixer = lru_mixer
            params = (lru_w_in[j], lru_conv_w[j], lru_conv_b[j], lru_w_gates[j], lru_b_gates[j], lru_lambda[j], lru_w_out[j])
            states = ((jnp.zeros((BATCH, D_RNN), state_lru_h.dtype),
                       jnp.zeros((BATCH, CONV_W - 1, D_RNN), state_lru_conv.dtype)),
                      (state_lru_h[j], state_lru_conv[j]))
        elif m == 2:
            mixer = ret_mixer
            params = (ret_w_in[j], ret_gn_w[j], ret_w_out[j])
            states = ((jnp.zeros((BATCH, H_C, DK_C, DV_C), state_ret_S.dtype),), (state_ret_S[j],))
        else:
            mixer = moba_mixer
            params = (moba_w_in[j], moba_w_out[j])
            k_past = cache_k[j, page_table].reshape(n_seq, past_len, H_D, HD_D)
            v_past = cache_v[j, page_table].reshape(n_seq, past_len, H_D, HD_D)
            states = ((), (k_past, v_past))
        peer_params = (peer_wq[i], peer_keys[i], peer_u[i], peer_v[i])
        for grp in range(2):
            xs[grp], ns = trunk_layer(xs[grp], c_act[grp], w_ada[i], b_ada[i], ln_g[i], ln_b[i],
                                      peer_params, mixer, params, states[grp])
            new_states[(m, grp)].append(ns)

    def stacked(m, grp):
        return [jnp.stack(parts) for parts in zip(*new_states[(m, grp)])]

    gdn_S_p, gdn_conv_p = stacked(0, 0)
    gdn_S_s, gdn_conv_s = stacked(0, 1)
    lru_h_p, lru_conv_p = stacked(1, 0)
    lru_h_s, lru_conv_s = stacked(1, 1)
    (ret_S_p,) = stacked(2, 0)
    (ret_S_s,) = stacked(2, 1)
    k_p, v_p = stacked(3, 0)
    k_s, v_s = stacked(3, 1)
    return (xs[0], xs[1], gdn_S_p, gdn_S_s, gdn_conv_p, gdn_conv_s, lru_h_p, lru_h_s, lru_conv_p, lru_conv_s,
            ret_S_p, ret_S_s, k_p, k_s, v_p, v_s)
```

```python
import functools
import math

import jax
import jax.numpy as jnp
from jax import lax
from jax.experimental import pallas as pl
from jax.experimental.pallas import tpu as pltpu

F32 = jnp.float32
BF16 = jnp.bfloat16

DEPTH = 4
ALPHA = (2.0 * DEPTH) ** 0.25
CONV_W = 4
LN_EPS = 1e-5
NORM_EPS = 1e-6
GDN_CHUNK = 64
RET_CHUNK = 64
LRU_C = 8.0
MOBA_BLOCK = 256
MOBA_TOPK = 3
P_TOPK = 16
N_KEYS = 128
LANE = 128
SUBLANE = 8
VMEM_LIMIT = 56 * 1024 * 1024
NEG_BIG = -1e30

_NT = (((1,), (1,)), ((), ()))
_TN = (((0,), (0,)), ((), ()))


def _tile(n, cap, align):
    for t in range(min(cap, n) // align * align, 0, -align):
        if n % t == 0:
            return t
    return n


def _params(sem):
    return pltpu.CompilerParams(dimension_semantics=sem, vmem_limit_bytes=VMEM_LIMIT)


def _bdot(a, b):
    return jnp.dot(a.astype(BF16), b.astype(BF16), preferred_element_type=F32)


def _bdot_nt(a, b):
    return lax.dot_general(a.astype(BF16), b.astype(BF16), _NT, preferred_element_type=F32)


def _bdot_tn(a, b):
    return lax.dot_general(a.astype(BF16), b.astype(BF16), _TN, preferred_element_type=F32)


def _fdot(a, b):
    return jnp.dot(a, b, preferred_element_type=F32, precision=lax.Precision.HIGHEST)


def _gelu_tanh(x):
    return 0.5 * x * (1.0 + jnp.tanh(math.sqrt(2.0 / math.pi) * (x + 0.044715 * (x * x * x))))


def _sigmoid(x):
    return 1.0 / (1.0 + jnp.exp(-x))


def _silu(x):
    return x * _sigmoid(x)


def _mm_kernel(a_ref, b_ref, *rest, nk, has_bias):
    if has_bias:
        bias_ref, o_ref, acc_ref = rest
    else:
        o_ref, acc_ref = rest
    k = pl.program_id(2)

    @pl.when(k == 0)
    def _():
        acc_ref[...] = jnp.zeros_like(acc_ref)

    acc_ref[...] += _bdot(a_ref[...], b_ref[...])

    @pl.when(k == nk - 1)
    def _():
        r = acc_ref[...]
        if has_bias:
            r = r + bias_ref[...]
        o_ref[...] = r.astype(o_ref.dtype)


def matmul(a, b, *, n_cols=None, bias=None, out_dtype=F32):
    M, K = a.shape
    N = b.shape[1] if n_cols is None else n_cols
    tm = _tile(M, 1024, SUBLANE)
    tn = _tile(N, 1024 if M > 64 else 2048, LANE)
    tk = _tile(K, 512, LANE)
    nk = K // tk
    in_specs = [pl.BlockSpec((tm, tk), lambda i, j, k: (i, k)),
                pl.BlockSpec((tk, tn), lambda i, j, k: (k, j))]
    args = [a, b]
    if bias is not None:
        in_specs.append(pl.BlockSpec((1, tn), lambda i, j, k: (0, j)))
        args.append(bias.reshape(1, -1))
    return pl.pallas_call(
        functools.partial(_mm_kernel, nk=nk, has_bias=bias is not None),
        grid=(M // tm, N // tn, nk),
        in_specs=in_specs,
        out_specs=pl.BlockSpec((tm, tn), lambda i, j, k: (i, j)),
        out_shape=jax.ShapeDtypeStruct((M, N), out_dtype),
        scratch_shapes=[pltpu.VMEM((tm, tn), F32)],
        compiler_params=_params(("parallel", "parallel", "arbitrary")),
    )(*args)


def _modulate_kernel(x_ref, sc_ref, sh_ref, h_ref):
    h_ref[0] = (x_ref[0] * (1.0 + sc_ref[0]) + sh_ref[0]).astype(h_ref.dtype)


def modulate(x, sc, sh):
    B, L, D = x.shape
    tl = min(L, 256)
    row = pl.BlockSpec((1, tl, D), lambda b, t: (b, t, 0))
    vec = pl.BlockSpec((1, 1, D), lambda b, t: (b, 0, 0))
    return pl.pallas_call(
        _modulate_kernel, grid=(B, L // tl), in_specs=[row, vec, vec], out_specs=row,
        out_shape=jax.ShapeDtypeStruct((B, L, D), BF16),
        compiler_params=_params(("parallel", "parallel")),
    )(x, sc, sh)


def _ln_kernel(x_ref, y_ref, g_ref, sc_ref, sh_ref, lg_ref, lb_ref, xo_ref, ho_ref):
    z = ALPHA * x_ref[0] + (1.0 + g_ref[0]) * y_ref[0]
    mu = jnp.mean(z, axis=-1, keepdims=True)
    zc = z - mu
    var = jnp.mean(zc * zc, axis=-1, keepdims=True)
    xn = zc * lax.rsqrt(var + LN_EPS) * lg_ref[...] + lb_ref[...]
    xo_ref[0] = xn
    ho_ref[0] = (xn * (1.0 + sc_ref[0]) + sh_ref[0]).astype(ho_ref.dtype)


def ln_residual(x, y, gate, sc_next, sh_next, ln_g, ln_b):
    B, L, D = x.shape
    tl = min(L, 256)
    row = pl.BlockSpec((1, tl, D), lambda b, t: (b, t, 0))
    vec = pl.BlockSpec((1, 1, D), lambda b, t: (b, 0, 0))
    par = pl.BlockSpec((1, D), lambda b, t: (0, 0))
    return pl.pallas_call(
        _ln_kernel, grid=(B, L // tl),
        in_specs=[row, row, vec, vec, vec, par, par], out_specs=[row, row],
        out_shape=[jax.ShapeDtypeStruct((B, L, D), F32), jax.ShapeDtypeStruct((B, L, D), BF16)],
        compiler_params=_params(("parallel", "parallel")),
    )(x, y, gate, sc_next, sh_next, ln_g.reshape(1, D), ln_b.reshape(1, D))


def _unit_lower_inverse(m, c):
    eye = (lax.broadcasted_iota(jnp.int32, (c, c), 0) == lax.broadcasted_iota(jnp.int32, (c, c), 1)).astype(F32)
    t = eye - m
    p = m
    for _ in range(max(int(math.log2(c)) - 1, 0)):
        p = _fdot(p, p)
        t = t + _fdot(t, p)
    return t


def _gdn_kernel(q_ref, k_ref, v_ref, z_ref, gc_ref, gr_ref, bt_ref, s0_ref, nw_ref, y_ref, s_ref, *, hb, c, dk, dv):
    n = pl.program_id(2)

    @pl.when(n == 0)
    def _():
        s_ref[...] = s0_ref[...]

    row = lax.broadcasted_iota(jnp.int32, (c, c), 0)
    col = lax.broadcasted_iota(jnp.int32, (c, c), 1)
    tri = row >= col
    strict = row > col
    for h in range(hb):
        qh = q_ref[0, :, h * dk:(h + 1) * dk]
        kh = k_ref[0, :, h * dk:(h + 1) * dk]
        vh = v_ref[0, :, h * dv:(h + 1) * dv]
        gcl = gc_ref[0, h]
        grw = gr_ref[0, h, 0]
        bt = bt_ref[0, h]
        decay = jnp.where(tri, jnp.exp(jnp.where(tri, gcl - grw, 0.0)), 0.0)
        kb = kh * bt
        m = jnp.where(strict, _bdot_nt(kb, kh) * decay, 0.0)
        tinv = _unit_lower_inverse(m, c)
        u = _fdot(tinv, vh * bt)
        w = _fdot(tinv, kb * jnp.exp(gcl))
        qk = jnp.where(tri, _bdot_nt(qh, kh) * decay, 0.0)
        S = s_ref[0, h]
        v_new = u - _bdot(w, S)
        o = _bdot(qh * jnp.exp(gcl), S) + _bdot(qk, v_new)
        g_last = gcl[c - 1:c, :]
        s_ref[0, h] = S * jnp.exp(g_last) + _bdot_tn(kh * jnp.exp(g_last - gcl), v_new)
        o = o * lax.rsqrt(jnp.mean(o * o, axis=-1, keepdims=True) + NORM_EPS) * nw_ref[...]
        zh = z_ref[0, :, h * dv:(h + 1) * dv]
        y_ref[0, :, h * dv:(h + 1) * dv] = (o * _silu(zh)).astype(y_ref.dtype)


def gdn_core(q, k, v, z, gcs, beta, S0, norm_w, c):
    B, L, _ = q.shape
    H, dk, dv = S0.shape[1:]
    hb = 8
    nc = L // c
    gt = jnp.transpose(gcs, (0, 2, 1))
    gcol = gt[..., None]
    grow = gt.reshape(B, H, nc, 1, c)
    bcol = jnp.transpose(beta, (0, 2, 1))[..., None]
    qk_spec = pl.BlockSpec((1, c, hb * dk), lambda b, g, n: (b, n, g))
    v_spec = pl.BlockSpec((1, c, hb * dv), lambda b, g, n: (b, n, g))
    col_spec = pl.BlockSpec((1, hb, c, 1), lambda b, g, n: (b, g, n, 0))
    row_spec = pl.BlockSpec((1, hb, 1, 1, c), lambda b, g, n: (b, g, n, 0, 0))
    s_spec = pl.BlockSpec((1, hb, dk, dv), lambda b, g, n: (b, g, 0, 0))
    return pl.pallas_call(
        functools.partial(_gdn_kernel, hb=hb, c=c, dk=dk, dv=dv),
        grid=(B, H // hb, nc),
        in_specs=[qk_spec, qk_spec, v_spec, v_spec, col_spec, row_spec, col_spec, s_spec,
                  pl.BlockSpec((1, dv), lambda b, g, n: (0, 0))],
        out_specs=[v_spec, s_spec],
        out_shape=[jax.ShapeDtypeStruct((B, L, H * dv), BF16), jax.ShapeDtypeStruct((B, H, dk, dv), F32)],
        compiler_params=_params(("parallel", "parallel", "arbitrary")),
    )(q, k, v, z, gcol, grow, bcol, S0, norm_w.reshape(1, dv))


def _ret_kernel(q_ref, k_ref, v_ref, gz_ref, gc_ref, gr_ref, s0_ref, gn_ref, y_ref, s_ref, *, hb, c, dk, dv):
    n = pl.program_id(2)

    @pl.when(n == 0)
    def _():
        s_ref[...] = s0_ref[...]

    row = lax.broadcasted_iota(jnp.int32, (c, c), 0)
    col = lax.broadcasted_iota(jnp.int32, (c, c), 1)
    tri = row >= col
    for h in range(hb):
        qh = q_ref[0, :, h * dk:(h + 1) * dk]
        kh = k_ref[0, :, h * dk:(h + 1) * dk] * (dk ** -0.5)
        vh = v_ref[0, :, h * dv:(h + 1) * dv]
        gcl = gc_ref[0, h]
        grw = gr_ref[0, h, 0]
        decay = jnp.where(tri, jnp.exp(jnp.where(tri, gcl - grw, 0.0)), 0.0)
        S = s_ref[0, h]
        s = _bdot_nt(qh, kh) * decay
        o = _bdot(s, vh) + _bdot(qh * jnp.exp(gcl), S)
        g_last = gcl[c - 1:c, :]
        s_ref[0, h] = jnp.exp(g_last) * S + _bdot_tn(kh * jnp.exp(g_last - gcl), vh)
        mu = jnp.mean(o, axis=-1, keepdims=True)
        oc = o - mu
        var = jnp.mean(oc * oc, axis=-1, keepdims=True)
        on = oc * lax.rsqrt(var + NORM_EPS) * gn_ref[h]
        gz = gz_ref[0, :, h * dv:(h + 1) * dv]
        y_ref[0, :, h * dv:(h + 1) * dv] = (_silu(gz) * on).astype(y_ref.dtype)


def ret_core(q, k, v, gz, gcs, S0, gn_w, c):
    B, L, _ = q.shape
    H, dk, dv = S0.shape[1:]
    hb = 4
    nc = L // c
    gt = jnp.transpose(gcs, (0, 2, 1))
    gcol = gt[..., None]
    grow = gt.reshape(B, H, nc, 1, c)
    qk_spec = pl.BlockSpec((1, c, hb * dk), lambda b, g, n: (b, n, g))
    v_spec = pl.BlockSpec((1, c, hb * dv), lambda b, g, n: (b, n, g))
    col_spec = pl.BlockSpec((1, hb, c, 1), lambda b, g, n: (b, g, n, 0))
    row_spec = pl.BlockSpec((1, hb, 1, 1, c), lambda b, g, n: (b, g, n, 0, 0))
    s_spec = pl.BlockSpec((1, hb, dk, dv), lambda b, g, n: (b, g, 0, 0))
    return pl.pallas_call(
        functools.partial(_ret_kernel, hb=hb, c=c, dk=dk, dv=dv),
        grid=(B, H // hb, nc),
        in_specs=[qk_spec, qk_spec, v_spec, v_spec, col_spec, row_spec, s_spec,
                  pl.BlockSpec((hb, 1, dv), lambda b, g, n: (g, 0, 0))],
        out_specs=[v_spec, s_spec],
        out_shape=[jax.ShapeDtypeStruct((B, L, H * dv), BF16), jax.ShapeDtypeStruct((B, H, dk, dv), F32)],
        compiler_params=_params(("parallel", "parallel", "arbitrary")),
    )(q, k, v, gz, gcol, grow, S0, gn_w.reshape(H, 1, dv))


def _lru_kernel(x_ref, gi_ref, wg_ref, bg_ref, sp_ref, h0_ref, y_ref, hl_ref, a_s, b_s, hs_s, h_s, *, tl, seq):
    t = pl.program_id(2)
    x = x_ref[0]
    r = _sigmoid(_bdot(x, wg_ref[0, 0]) + bg_ref[0])
    ig = _sigmoid(_bdot(x, wg_ref[1, 0]) + bg_ref[1])
    log_a = -LRU_C * r * sp_ref[...]
    a = jnp.exp(log_a)
    b = jnp.sqrt(1.0 - jnp.exp(2.0 * log_a)) * (ig * x)
    if seq:
        @pl.when(t == 0)
        def _():
            h_s[...] = h0_ref[0]

        a_s[...] = a
        b_s[...] = b

        def body(i, h):
            h = a_s[pl.ds(i, 1), :] * h + b_s[pl.ds(i, 1), :]
            hs_s[pl.ds(i, 1), :] = h
            return h

        h = lax.fori_loop(0, tl, body, h_s[...])
        h_s[...] = h
        hl_ref[0] = h
        hs = hs_s[...]
    else:
        hs = a * h0_ref[0] + b
        hl_ref[0] = hs
    y_ref[0] = (hs * _gelu_tanh(gi_ref[0])).astype(y_ref.dtype)


def lru_core(xc, gate_in, wg2, bg, sp, h0, seq):
    B, L, C = xc.shape
    sb = wg2.shape[-1]
    ns = C // sb
    tl = min(L, 256)
    x_spec = pl.BlockSpec((1, tl, sb), lambda b, s, t: (b, t, s))
    h_rows = 1 if seq else tl
    h_spec = pl.BlockSpec((1, h_rows, sb), (lambda b, s, t: (b, 0, s)) if seq else (lambda b, s, t: (b, t, s)))
    return pl.pallas_call(
        functools.partial(_lru_kernel, tl=tl, seq=seq),
        grid=(B, ns, L // tl),
        in_specs=[x_spec, x_spec,
                  pl.BlockSpec((2, 1, sb, sb), lambda b, s, t: (0, s, 0, 0)),
                  pl.BlockSpec((2, 1, sb), lambda b, s, t: (0, 0, s)),
                  pl.BlockSpec((1, sb), lambda b, s, t: (0, s)),
                  h_spec],
        out_specs=[x_spec, h_spec],
        out_shape=[jax.ShapeDtypeStruct((B, L, C), BF16), jax.ShapeDtypeStruct((B, h_rows if seq else L, C), F32)],
        scratch_shapes=[pltpu.VMEM((tl, sb), F32), pltpu.VMEM((tl, sb), F32), pltpu.VMEM((tl, sb), F32),
                        pltpu.VMEM((1, sb), F32)],
        compiler_params=_params(("parallel", "parallel", "arbitrary")),
    )(xc, gate_in, wg2, bg, sp, h0)


def _moba_prompt_kernel(q_ref, k_ref, v_ref, sl_ref, o_ref, km_s, *, nb, blk, hd):
    qi = pl.program_id(2)

    @pl.when(qi == 0)
    def _():
        km_s[...] = jnp.zeros_like(km_s)
        for n in range(nb):
            km_s[n:n + 1, :] = jnp.mean(k_ref[0, n * blk:(n + 1) * blk, :], axis=0, keepdims=True)

    q = q_ref[0] * (hd ** -0.5)
    lane = lax.broadcasted_iota(jnp.int32, (blk, LANE), 1)
    gate = jnp.where(lane < qi, _bdot_nt(q, km_s[...]), -jnp.inf)
    sel = jnp.zeros((blk, LANE), F32)
    for _ in range(min(MOBA_TOPK, nb)):
        gmax = jnp.max(gate, axis=-1, keepdims=True)
        imax = jnp.min(jnp.where(gate == gmax, lane, LANE), axis=-1, keepdims=True)
        hit = (lane == imax) & (gmax > -jnp.inf)
        sel = jnp.where(hit, 1.0, sel)
        gate = jnp.where(lane == imax, -jnp.inf, gate)
    slope = sl_ref[0]
    rowi = lax.broadcasted_iota(jnp.int32, (blk, blk), 0)
    coli = lax.broadcasted_iota(jnp.int32, (blk, blk), 1)
    s_blocks = []
    for n in range(nb):
        s = _bdot_nt(q, k_ref[0, n * blk:(n + 1) * blk, :])
        dist = (qi - n) * blk + rowi - coli
        ok = ((sel[:, n:n + 1] > 0.0) | (qi == n)) & (dist >= 0)
        s_blocks.append(jnp.where(ok, s - slope * dist.astype(F32), NEG_BIG))
    m = s_blocks[0].max(axis=-1, keepdims=True)
    for n in range(1, nb):
        m = jnp.maximum(m, s_blocks[n].max(axis=-1, keepdims=True))
    l = jnp.zeros((blk, 1), F32)
    acc = jnp.zeros((blk, hd), F32)
    for n in range(nb):
        p = jnp.where(s_blocks[n] > 0.5 * NEG_BIG, jnp.exp(s_blocks[n] - m), 0.0)
        l = l + p.sum(axis=-1, keepdims=True)
        acc = acc + _bdot(p, v_ref[0, n * blk:(n + 1) * blk, :])
    o_ref[0] = (acc / l).astype(o_ref.dtype)


def moba_prompt(q, k, v, slopes):
    B, L, D = q.shape
    H = slopes.shape[0]
    hd = D // H
    blk = MOBA_BLOCK
    nb = L // blk
    q_spec = pl.BlockSpec((1, blk, hd), lambda b, h, i: (b, i, h))
    kv_spec = pl.BlockSpec((1, L, hd), lambda b, h, i: (b, 0, h))
    return pl.pallas_call(
        functools.partial(_moba_prompt_kernel, nb=nb, blk=blk, hd=hd),
        grid=(B, H, nb),
        in_specs=[q_spec, kv_spec, kv_spec, pl.BlockSpec((1, 1, 1), lambda b, h, i: (h, 0, 0))],
        out_specs=q_spec,
        out_shape=jax.ShapeDtypeStruct((B, L, D), BF16),
        scratch_shapes=[pltpu.VMEM((LANE, hd), F32)],
        compiler_params=_params(("parallel", "parallel", "arbitrary")),
    )(q, k, v, slopes.reshape(H, 1, 1))


def _kblock_sum_kernel(pt_ref, k_ref, o_ref, *, ppb):
    p = pl.program_id(1)

    @pl.when(p % ppb == 0)
    def _():
        o_ref[...] = jnp.zeros_like(o_ref)

    o_ref[0, 0] += jnp.sum(k_ref[0, 0], axis=0)


def kblock_sums(cache_k, page_table, layer):
    n_seq, n_pages = page_table.shape
    _, _, page, H, hd = cache_k.shape
    ppb = MOBA_BLOCK // page
    grid_spec = pltpu.PrefetchScalarGridSpec(
        num_scalar_prefetch=1, grid=(n_seq, n_pages),
        in_specs=[pl.BlockSpec((1, 1, page, H, hd), lambda b, p, pt: (layer, pt[b, p], 0, 0, 0))],
        out_specs=pl.BlockSpec((1, 1, H, hd), lambda b, p, pt: (b, p // ppb, 0, 0)))
    return pl.pallas_call(
        functools.partial(_kblock_sum_kernel, ppb=ppb), grid_spec=grid_spec,
        out_shape=jax.ShapeDtypeStruct((n_seq, n_pages // ppb, H, hd), F32),
        compiler_params=_params(("parallel", "arbitrary")),
    )(page_table, cache_k)


def _moba_step_kernel(pt_ref, q_ref, kn_ref, vn_ref, sel_ref, sl_ref, k_ref, v_ref, o_ref, m_s, l_s, acc_s,
                      *, page, n_pages, hd):
    p = pl.program_id(1)
    q = q_ref[0] * (hd ** -0.5)

    @pl.when(p == 0)
    def _():
        m_s[...] = jnp.sum(q * kn_ref[0], axis=-1, keepdims=True)
        l_s[...] = jnp.ones_like(l_s)
        acc_s[...] = vn_ref[0]

    k = k_ref[0, 0]
    s = jnp.sum(k * q[None], axis=-1, keepdims=True)
    tok = lax.broadcasted_iota(jnp.int32, s.shape, 0)
    dist = (n_pages * page - (p * page + tok)).astype(F32)
    ok = sel_ref[0, 0][None] > 0.0
    s = jnp.where(ok, s - sl_ref[...][None] * dist, NEG_BIG)
    m_old = m_s[...]
    m_new = jnp.maximum(m_old, jnp.max(s, axis=0))
    alpha = jnp.exp(m_old - m_new)
    pr = jnp.where(ok, jnp.exp(s - m_new[None]), 0.0)
    l_s[...] = alpha * l_s[...] + jnp.sum(pr, axis=0)
    acc_s[...] = alpha * acc_s[...] + jnp.sum(pr * v_ref[0, 0], axis=0)
    m_s[...] = m_new

    @pl.when(p == n_pages - 1)
    def _():
        o_ref[0] = (acc_s[...] / l_s[...]).astype(o_ref.dtype)


def moba_step(q, k_new, v_new, sel_pages, slopes, cache_k, cache_v, page_table, layer):
    n_seq, n_pages = page_table.shape
    _, _, page, H, hd = cache_k.shape
    vec = pl.BlockSpec((1, H, hd), lambda b, p, pt: (b, 0, 0))
    kv = pl.BlockSpec((1, 1, page, H, hd), lambda b, p, pt: (layer, pt[b, p], 0, 0, 0))
    grid_spec = pltpu.PrefetchScalarGridSpec(
        num_scalar_prefetch=1, grid=(n_seq, n_pages),
        in_specs=[vec, vec, vec,
                  pl.BlockSpec((1, 1, H, 1), lambda b, p, pt: (b, p, 0, 0)),
                  pl.BlockSpec((H, 1), lambda b, p, pt: (0, 0)),
                  kv, kv],
        out_specs=vec,
        scratch_shapes=[pltpu.VMEM((H, 1), F32), pltpu.VMEM((H, 1), F32), pltpu.VMEM((H, hd), F32)])
    return pl.pallas_call(
        functools.partial(_moba_step_kernel, page=page, n_pages=n_pages, hd=hd), grid_spec=grid_spec,
        out_shape=jax.ShapeDtypeStruct((n_seq, H, hd), BF16),
        compiler_params=_params(("parallel", "arbitrary")),
    )(page_table, q, k_new, v_new, sel_pages, slopes.reshape(H, 1), cache_k, cache_v)


def _peer_scores_kernel(q_ref, keys_ref, s_ref, *, ng, dsub):
    for g in range(ng):
        s_ref[:, g * N_KEYS:(g + 1) * N_KEYS] = _bdot_nt(q_ref[:, g * dsub:(g + 1) * dsub], keys_ref[g])


def peer_scores(q, keys):
    T = q.shape[0]
    G, nk, dsub = keys.shape
    tm = min(T, 512)
    return pl.pallas_call(
        functools.partial(_peer_scores_kernel, ng=G, dsub=dsub),
        grid=(T // tm,),
        in_specs=[pl.BlockSpec((tm, G * dsub), lambda i: (i, 0)), pl.BlockSpec((G, nk, dsub), lambda i: (0, 0, 0))],
        out_specs=pl.BlockSpec((tm, G * nk), lambda i: (i, 0)),
        out_shape=jax.ShapeDtypeStruct((T, G * nk), F32),
        compiler_params=_params(("parallel",)),
    )(q, keys)


def _peer_gates_kernel(g_ref, i_ref, j_ref, o_ref, *, tb):
    iota = lax.broadcasted_iota(jnp.int32, (tb, N_KEYS, LANE), 1)
    a = jnp.where(i_ref[...] == iota, g_ref[...], 0.0)
    a_hi = a.astype(BF16)
    a_lo = (a - a_hi.astype(F32)).astype(BF16)
    bt = jnp.where(j_ref[...] == iota, 1.0, 0.0).astype(BF16)
    dn = (((2,), (2,)), ((0,), (0,)))
    o_ref[...] = (lax.dot_general(a_hi, bt, dn, preferred_element_type=F32)
                  + lax.dot_general(a_lo, bt, dn, preferred_element_type=F32))


def peer_gate_matrix(gates, rows, cols):
    T, slots = gates.shape
    assert slots == LANE
    tb = min(T, 32)
    spec = pl.BlockSpec((tb, 1, slots), lambda i: (i, 0, 0))
    out = pl.pallas_call(
        functools.partial(_peer_gates_kernel, tb=tb),
        grid=(T // tb,), in_specs=[spec, spec, spec],
        out_specs=pl.BlockSpec((tb, N_KEYS, N_KEYS), lambda i: (i, 0, 0)),
        out_shape=jax.ShapeDtypeStruct((T, N_KEYS, N_KEYS), F32),
        compiler_params=_params(("parallel",)),
    )(gates.reshape(T, 1, slots), rows.reshape(T, 1, slots), cols.reshape(T, 1, slots))
    return out.reshape(T, N_KEYS * N_KEYS)


def _peer_experts_kernel(x_ref, g_ref, u_ref, v_ref, o_ref):
    e = pl.program_id(1)

    @pl.when(e == 0)
    def _():
        o_ref[...] = jnp.zeros_like(o_ref)

    act = _gelu_tanh(_bdot_nt(x_ref[...], u_ref[...]))
    o_ref[...] += _bdot(g_ref[...] * act, v_ref[...])


def peer_experts(x, gmat, u, v):
    T, D = x.shape
    E = u.shape[0]
    tm = min(T, 512)
    te = 256
    return pl.pallas_call(
        _peer_experts_kernel,
        grid=(T // tm, E // te),
        in_specs=[pl.BlockSpec((tm, D), lambda i, e: (i, 0)), pl.BlockSpec((tm, te), lambda i, e: (i, e)),
                  pl.BlockSpec((te, D), lambda i, e: (e, 0)), pl.BlockSpec((te, D), lambda i, e: (e, 0))],
        out_specs=pl.BlockSpec((tm, D), lambda i, e: (i, 0)),
        out_shape=jax.ShapeDtypeStruct((T, D), F32),
        compiler_params=_params(("parallel", "arbitrary")),
    )(x, gmat, u, v)


def peer_ffn(h, wq, keys, u, v):
    T = h.shape[0]
    n_heads = keys.shape[0]
    dsub = keys.shape[-1]
    q = matmul(h, wq)
    s = peer_scores(q, keys.reshape(n_heads * 2, N_KEYS, dsub)).reshape(T, n_heads, 2, N_KEYS)
    s_top, i_top = lax.top_k(s, P_TOPK)
    cand_s = (s_top[:, :, 0, :, None] + s_top[:, :, 1, None, :]).reshape(T, n_heads, P_TOPK * P_TOPK)
    best_s, best_pos = lax.top_k(cand_s, P_TOPK)
    rows = jnp.take_along_axis(i_top[:, :, 0], best_pos // P_TOPK, axis=-1)
    cols = jnp.take_along_axis(i_top[:, :, 1], best_pos % P_TOPK, axis=-1)
    gates = jax.nn.softmax(best_s, axis=-1)
    slots = n_heads * P_TOPK
    gmat = peer_gate_matrix(gates.reshape(T, slots), rows.reshape(T, slots).astype(jnp.int32),
                            cols.reshape(T, slots).astype(jnp.int32))
    return peer_experts(h, gmat, u, v)


def causal_dwconv(x, buf, w, b=None):
    L = x.shape[1]
    xp = jnp.concatenate([buf.astype(x.dtype), x], axis=1)
    y = xp[:, 0:L] * w[0]
    for j in range(1, CONV_W):
        y = y + xp[:, j:j + L] * w[j]
    if b is not None:
        y = y + b
    return y, xp[:, L:]


def _pad_rows(t, rows):
    return jnp.pad(t, ((0, 0), (0, rows - t.shape[1])) + ((0, 0),) * (t.ndim - 2))


def _chunk_cumsum(g, c):
    B, L, H = g.shape
    return jnp.cumsum(g.reshape(B, L // c, c, H), axis=2).reshape(B, L, H)


def gdn_mixer(h, state, params):
    S0, conv0 = state
    w_in, conv_w, a_log, dt_bias, norm_w, w_out = params
    B, L, D = h.shape
    H, dk, dv = S0.shape[1:]
    d_qk, d_v = H * dk, H * dv
    n_main = 2 * d_qk + 2 * d_v
    hf = h.reshape(B * L, D)
    proj = matmul(hf, w_in, n_cols=n_main).reshape(B, L, n_main)
    ba = matmul(hf, w_in[:, n_main:]).reshape(B, L, 2 * H)
    qkv, conv_new = causal_dwconv(proj[..., :2 * d_qk + d_v], conv0, conv_w)
    z = proj[..., 2 * d_qk + d_v:]
    qkv = jax.nn.silu(qkv)
    q = qkv[..., :d_qk].reshape(B, L, H, dk)
    k = qkv[..., d_qk:2 * d_qk].reshape(B, L, H, dk)
    v = qkv[..., 2 * d_qk:]
    q = (q * lax.rsqrt(jnp.sum(q * q, -1, keepdims=True) + NORM_EPS) * (dk ** -0.5)).reshape(B, L, d_qk)
    k = (k * lax.rsqrt(jnp.sum(k * k, -1, keepdims=True) + NORM_EPS)).reshape(B, L, d_qk)
    beta = jax.nn.sigmoid(ba[..., :H])
    g = -jnp.exp(a_log) * jax.nn.softplus(ba[..., H:] + dt_bias)
    c = math.gcd(L, GDN_CHUNK)
    if c < SUBLANE:
        Lp = SUBLANE
        q, k, v, z, g, beta = (_pad_rows(t, Lp) for t in (q, k, v, z, g, beta))
        c = Lp
    y, S = gdn_core(q, k, v, z, _chunk_cumsum(g, c), beta, S0, norm_w, c)
    out = matmul(y[:, :L].reshape(B * L, d_v), w_out).reshape(B, L, D)
    return out, (S, conv_new)


def lru_mixer(h, state, params):
    h0, conv0 = state
    w_in, conv_w, conv_b, w_gates, b_gates, lam, w_out = params
    B, L, D = h.shape
    C = h0.shape[-1]
    proj = matmul(h.reshape(B * L, D), w_in).reshape(B, L, 2 * C)
    gate_in, rec_in = proj[..., :C], proj[..., C:]
    xc, conv_new = causal_dwconv(rec_in, conv0, conv_w, conv_b)
    nb, bw = w_gates.shape[1], w_gates.shape[2]
    per = (bw * LANE // math.gcd(bw, LANE)) // bw
    wg = w_gates.reshape(2, nb // per, per, bw, bw)
    eye = jnp.eye(per, dtype=w_gates.dtype)
    wg2 = jnp.einsum('gspij,pq->gspiqj', wg, eye).reshape(2, nb // per, per * bw, per * bw)
    sp = jax.nn.softplus(-lam).reshape(1, C)
    bg = b_gates.reshape(2, 1, C)
    if L > 1:
        y, hl = lru_core(xc, gate_in, wg2, bg, sp, h0.reshape(B, 1, C), True)
        h_last = hl.reshape(B, C)
    else:
        y, hl = lru_core(xc.reshape(1, B, C), gate_in.reshape(1, B, C), wg2, bg, sp, h0.reshape(1, B, C), False)
        y = y.reshape(B, 1, C)
        h_last = hl.reshape(B, C)
    out = matmul(y.reshape(B * L, C), w_out).reshape(B, L, D)
    return out, (h_last, conv_new)


def ret_mixer(h, state, params):
    (S0,) = state
    w_in, gn_w, w_out = params
    B, L, D = h.shape
    H, dk, dv = S0.shape[1:]
    d_qk, d_v = H * dk, H * dv
    proj = matmul(h.reshape(B * L, D), w_in).reshape(B, L, 2 * d_qk + 2 * d_v)
    q, k = proj[..., :d_qk], proj[..., d_qk:2 * d_qk]
    v, gz = proj[..., 2 * d_qk:2 * d_qk + d_v], proj[..., 2 * d_qk + d_v:]
    log_gamma = jnp.log1p(-jnp.exp2(-5.0 - jnp.arange(H, dtype=F32)))
    c = math.gcd(L, RET_CHUNK)
    Lp = L
    if c < SUBLANE:
        Lp = c = SUBLANE
        q, k, v, gz = (_pad_rows(t, Lp) for t in (q, k, v, gz))
    pos = jnp.arange(Lp)
    steps = jnp.minimum(pos % c + 1, jnp.maximum(L - (pos // c) * c, 0)).astype(F32)
    gcs = jnp.broadcast_to(steps[None, :, None] * log_gamma[None, None, :], (B, Lp, H))
    y, S = ret_core(q, k, v, gz, gcs, S0, gn_w, c)
    out = matmul(y[:, :L].reshape(B * L, d_v), w_out).reshape(B, L, D)
    return out, (S,)


def moba_mixer(h, past, params, n_heads):
    w_in, w_out = params
    B, L, D = h.shape
    hd = D // n_heads
    proj = matmul(h.reshape(B * L, D), w_in).reshape(B, L, 3 * D)
    q, k, v = proj[..., :D], proj[..., D:2 * D], proj[..., 2 * D:]
    slopes = jnp.exp2(-8.0 * (jnp.arange(n_heads, dtype=F32) + 1.0) / n_heads)
    if past is None:
        o = moba_prompt(q, k, v, slopes)
    else:
        cache_k, cache_v, page_table, layer = past
        page = cache_k.shape[2]
        ppb = MOBA_BLOCK // page
        qh = q.reshape(B, n_heads, hd)
        k_mean = kblock_sums(cache_k, page_table, layer) / MOBA_BLOCK
        gate = jnp.einsum('bhd,bnhd->bhn', qh * (hd ** -0.5), k_mean)
        _, top_i = lax.top_k(gate, min(MOBA_TOPK, gate.shape[-1]))
        sel = jnp.sum(jax.nn.one_hot(top_i, gate.shape[-1], dtype=F32), axis=2)
        sel_pages = jnp.repeat(jnp.transpose(sel, (0, 2, 1)), ppb, axis=1)[..., None]
        o = moba_step(qh, k.reshape(B, n_heads, hd), v.reshape(B, n_heads, hd), sel_pages, slopes,
                      cache_k, cache_v, page_table, layer).reshape(B, L, D)
    out = matmul(o.reshape(B * L, D), w_out).reshape(B, L, D)
    return out, (k.reshape(B, L, n_heads, hd), v.reshape(B, L, n_heads, hd))


def kernel(x_prompt, x_sample, c_prompt, c_sample, state_gdn_S, state_gdn_conv, state_lru_h, state_lru_conv, state_ret_S, cache_k, cache_v, page_table, w_ada, b_ada, ln_g, ln_b, peer_wq, peer_keys, peer_u, peer_v, gdn_w_in, gdn_conv_w, gdn_a_log, gdn_dt_bias, gdn_norm_w, gdn_w_out, lru_w_in, lru_conv_w, lru_conv_b, lru_w_gates, lru_b_gates, lru_lambda, lru_w_out, ret_w_in, ret_gn_w, ret_w_out, moba_w_in, moba_w_out):
    depth, D = w_ada.shape[0], w_ada.shape[1]
    n_mixers = 4
    Bp, Bs = x_prompt.shape[0], x_sample.shape[0]
    H_A, DK_A, DV_A = state_gdn_S.shape[2:]
    conv_dim_a = state_gdn_conv.shape[-1]
    C_B = state_lru_h.shape[-1]
    H_C, DK_C, DV_C = state_ret_S.shape[2:]
    H_D = cache_k.shape[3]

    c_all = jax.nn.silu(jnp.concatenate([c_prompt, c_sample], axis=0))
    n_c = c_all.shape[0]
    c_pad = jnp.pad(c_all, ((0, -n_c % SUBLANE), (0, 0)))
    xs = [x_prompt, x_sample]
    groups = ((0, Bp), (Bp, Bs))
    new_states = {(m, grp): [] for m in range(n_mixers) for grp in range(2)}
    hs = [None, None]
    mod_all = [matmul(c_pad, w_ada[i], bias=b_ada[i]) for i in range(depth)]
    mods = [[[mod[s:s + n, None, r * D:(r + 1) * D] for r in range(6)] for s, n in groups] for mod in mod_all]
    for i in range(depth):
        m, j = i % n_mixers, i // n_mixers
        peer_params = (peer_wq[i], peer_keys[i], peer_u[i], peer_v[i])
        for grp in range(2):
            sh1, sc1, g1, sh2, sc2, g2 = mods[i][grp]
            x = xs[grp]
            B, L, _ = x.shape
            if i == 0:
                hs[grp] = modulate(x, sc1, sh1)
            h = hs[grp]
            if m == 0:
                params = (gdn_w_in[j], gdn_conv_w[j], gdn_a_log[j], gdn_dt_bias[j], gdn_norm_w[j], gdn_w_out[j])
                state = ((jnp.zeros((B, H_A, DK_A, DV_A), F32), jnp.zeros((B, CONV_W - 1, conv_dim_a), F32))
                         if grp == 0 else (state_gdn_S[j], state_gdn_conv[j]))
                y, ns = gdn_mixer(h, state, params)
            elif m == 1:
                params = (lru_w_in[j], lru_conv_w[j], lru_conv_b[j], lru_w_gates[j], lru_b_gates[j], lru_lambda[j],
                          lru_w_out[j])
                state = ((jnp.zeros((B, C_B), F32), jnp.zeros((B, CONV_W - 1, C_B), F32))
                         if grp == 0 else (state_lru_h[j], state_lru_conv[j]))
                y, ns = lru_mixer(h, state, params)
            elif m == 2:
                state = (jnp.zeros((B, H_C, DK_C, DV_C), F32),) if grp == 0 else (state_ret_S[j],)
                y, ns = ret_mixer(h, state, (ret_w_in[j], ret_gn_w[j], ret_w_out[j]))
            else:
                past = None if grp == 0 else (cache_k, cache_v, page_table, j)
                y, ns = moba_mixer(h, past, (moba_w_in[j], moba_w_out[j]), H_D)
            new_states[(m, grp)].append(ns)
            x, h2 = ln_residual(x, y, g1, sc2, sh2, ln_g[i, 0], ln_b[i, 0])
            y2 = peer_ffn(h2.reshape(B * L, D), *peer_params).reshape(B, L, D)
            if i + 1 < depth:
                sh_n, sc_n = mods[i + 1][grp][0], mods[i + 1][grp][1]
            else:
                sh_n = sc_n = jnp.zeros((B, 1, D), F32)
            xs[grp], hs[grp] = ln_residual(x, y2, g2, sc_n, sh_n, ln_g[i, 1], ln_b[i, 1])

    def stacked(m, grp):
        return [jnp.stack(parts) for parts in zip(*new_states[(m, grp)])]

    gdn_S_p, gdn_conv_p = stacked(0, 0)
    gdn_S_s, gdn_conv_s = stacked(0, 1)
    lru_h_p, lru_conv_p = stacked(1, 0)
    lru_h_s, lru_conv_s = stacked(1, 1)
    (ret_S_p,) = stacked(2, 0)
    (ret_S_s,) = stacked(2, 1)
    k_p, v_p = stacked(3, 0)
    k_s, v_s = stacked(3, 1)
    return (xs[0], xs[1], gdn_S_p, gdn_S_s, gdn_conv_p, gdn_conv_s, lru_h_p, lru_h_s, lru_conv_p, lru_conv_s,
            ret_S_p, ret_S_s, k_p, k_s, v_p, v_s)
```

```python
import functools
import math

import jax
import jax.numpy as jnp
from jax import lax
from jax.experimental import pallas as pl
from jax.experimental.pallas import tpu as pltpu

F32 = jnp.float32
BF16 = jnp.bfloat16

DEPTH = 4
ALPHA = (2.0 * DEPTH) ** 0.25
CONV_W = 4
LN_EPS = 1e-5
NORM_EPS = 1e-6
GDN_CHUNK = 64
RET_CHUNK = 64
LRU_C = 8.0
MOBA_BLOCK = 256
MOBA_TOPK = 3
P_TOPK = 16
N_KEYS = 128
LANE = 128
SUBLANE = 8
VMEM_LIMIT = 56 * 1024 * 1024
NEG_BIG = -1e30

_NT = (((1,), (1,)), ((), ()))
_TN = (((0,), (0,)), ((), ()))


def _tile(n, cap, align):
    for t in range(min(cap, n) // align * align, 0, -align):
        if n % t == 0:
            return t
    return n


def _params(sem):
    return pltpu.CompilerParams(dimension_semantics=sem, vmem_limit_bytes=VMEM_LIMIT)


def _bdot(a, b):
    return jnp.dot(a.astype(BF16), b.astype(BF16), preferred_element_type=F32)


def _bdot_nt(a, b):
    return lax.dot_general(a.astype(BF16), b.astype(BF16), _NT, preferred_element_type=F32)


def _bdot_tn(a, b):
    return lax.dot_general(a.astype(BF16), b.astype(BF16), _TN, preferred_element_type=F32)


def _fdot(a, b):
    return jnp.dot(a, b, preferred_element_type=F32, precision=lax.Precision.HIGHEST)


def _gelu_tanh(x):
    return 0.5 * x * (1.0 + jnp.tanh(math.sqrt(2.0 / math.pi) * (x + 0.044715 * (x * x * x))))


def _sigmoid(x):
    return 1.0 / (1.0 + jnp.exp(-x))


def _silu(x):
    return x * _sigmoid(x)


def _mm_kernel(a_ref, b_ref, *rest, nk, has_bias):
    if has_bias:
        bias_ref, o_ref, acc_ref = rest
    else:
        o_ref, acc_ref = rest
    k = pl.program_id(2)

    @pl.when(k == 0)
    def _():
        acc_ref[...] = jnp.zeros_like(acc_ref)

    acc_ref[...] += _bdot(a_ref[...], b_ref[...])

    @pl.when(k == nk - 1)
    def _():
        r = acc_ref[...]
        if has_bias:
            r = r + bias_ref[...]
        o_ref[...] = r.astype(o_ref.dtype)


def matmul(a, b, *, n_cols=None, bias=None, out_dtype=F32):
    M, K = a.shape
    N = b.shape[1] if n_cols is None else n_cols
    tm = _tile(M, 1024, SUBLANE)
    tn = _tile(N, 1024 if M > 64 else 2048, LANE)
    tk = _tile(K, 512, LANE)
    nk = K // tk
    in_specs = [pl.BlockSpec((tm, tk), lambda i, j, k: (i, k)),
                pl.BlockSpec((tk, tn), lambda i, j, k: (k, j))]
    args = [a, b]
    if bias is not None:
        in_specs.append(pl.BlockSpec((1, tn), lambda i, j, k: (0, j)))
        args.append(bias.reshape(1, -1))
    return pl.pallas_call(
        functools.partial(_mm_kernel, nk=nk, has_bias=bias is not None),
        grid=(M // tm, N // tn, nk),
        in_specs=in_specs,
        out_specs=pl.BlockSpec((tm, tn), lambda i, j, k: (i, j)),
        out_shape=jax.ShapeDtypeStruct((M, N), out_dtype),
        scratch_shapes=[pltpu.VMEM((tm, tn), F32)],
        compiler_params=_params(("parallel", "parallel", "arbitrary")),
    )(*args)


def _mm_pair_kernel(ap_ref, as_ref, b_ref, op_ref, os_ref, b_s):
    @pl.when(pl.program_id(1) == 0)
    def _():
        b_s[...] = b_ref[...].astype(BF16)
        os_ref[...] = jnp.dot(as_ref[...].astype(BF16), b_s[...], preferred_element_type=F32)

    op_ref[...] = jnp.dot(ap_ref[...].astype(BF16), b_s[...], preferred_element_type=F32)


def matmul_pair(a_p, a_s, b, n_cols=None):
    Mp, K = a_p.shape
    Ms = a_s.shape[0]
    N = b.shape[1] if n_cols is None else n_cols
    wide = K <= 5120
    tm = _tile(Mp, 1024 if wide else 512, SUBLANE)
    tn = _tile(N, 512 if wide else 256, LANE)
    return pl.pallas_call(
        _mm_pair_kernel,
        grid=(N // tn, Mp // tm),
        in_specs=[pl.BlockSpec((tm, K), lambda j, i: (i, 0)), pl.BlockSpec((Ms, K), lambda j, i: (0, 0)),
                  pl.BlockSpec((K, tn), lambda j, i: (0, j))],
        out_specs=[pl.BlockSpec((tm, tn), lambda j, i: (i, j)), pl.BlockSpec((Ms, tn), lambda j, i: (0, j))],
        out_shape=[jax.ShapeDtypeStruct((Mp, N), F32), jax.ShapeDtypeStruct((Ms, N), F32)],
        scratch_shapes=[pltpu.VMEM((K, tn), BF16)],
        compiler_params=_params(("parallel", "arbitrary")),
    )(a_p, a_s, b)


def proj_pair(xs, w, n_cols=None):
    (Bp, Lp, K), (Bs, Ls, _) = xs[0].shape, xs[1].shape
    o_p, o_s = matmul_pair(xs[0].reshape(Bp * Lp, K), xs[1].reshape(Bs * Ls, K), w, n_cols)
    return o_p.reshape(Bp, Lp, -1), o_s.reshape(Bs, Ls, -1)


def _modulate_kernel(x_ref, sc_ref, sh_ref, h_ref):
    h_ref[0] = (x_ref[0] * (1.0 + sc_ref[0]) + sh_ref[0]).astype(h_ref.dtype)


def modulate(x, sc, sh):
    B, L, D = x.shape
    tl = min(L, 256)
    row = pl.BlockSpec((1, tl, D), lambda b, t: (b, t, 0))
    vec = pl.BlockSpec((1, 1, D), lambda b, t: (b, 0, 0))
    return pl.pallas_call(
        _modulate_kernel, grid=(B, L // tl), in_specs=[row, vec, vec], out_specs=row,
        out_shape=jax.ShapeDtypeStruct((B, L, D), BF16),
        compiler_params=_params(("parallel", "parallel")),
    )(x, sc, sh)


def _ln_kernel(x_ref, y_ref, g_ref, sc_ref, sh_ref, lg_ref, lb_ref, xo_ref, ho_ref):
    z = ALPHA * x_ref[0] + (1.0 + g_ref[0]) * y_ref[0]
    mu = jnp.mean(z, axis=-1, keepdims=True)
    zc = z - mu
    var = jnp.mean(zc * zc, axis=-1, keepdims=True)
    xn = zc * lax.rsqrt(var + LN_EPS) * lg_ref[...] + lb_ref[...]
    xo_ref[0] = xn
    ho_ref[0] = (xn * (1.0 + sc_ref[0]) + sh_ref[0]).astype(ho_ref.dtype)


def ln_residual(x, y, gate, sc_next, sh_next, ln_g, ln_b):
    B, L, D = x.shape
    tl = min(L, 256)
    row = pl.BlockSpec((1, tl, D), lambda b, t: (b, t, 0))
    vec = pl.BlockSpec((1, 1, D), lambda b, t: (b, 0, 0))
    par = pl.BlockSpec((1, D), lambda b, t: (0, 0))
    return pl.pallas_call(
        _ln_kernel, grid=(B, L // tl),
        in_specs=[row, row, vec, vec, vec, par, par], out_specs=[row, row],
        out_shape=[jax.ShapeDtypeStruct((B, L, D), F32), jax.ShapeDtypeStruct((B, L, D), BF16)],
        compiler_params=_params(("parallel", "parallel")),
    )(x, y, gate, sc_next, sh_next, ln_g.reshape(1, D), ln_b.reshape(1, D))


def _split_bf16(a):
    hi = a.astype(BF16)
    return hi, (a - hi.astype(F32)).astype(BF16)


def _dot3(a, b):
    ah, al = _split_bf16(a)
    bh, bl = _split_bf16(b)
    dot = functools.partial(jnp.dot, preferred_element_type=F32)
    return dot(ah, bh) + (dot(ah, bl) + dot(al, bh))


def _head_blocks(x, row_head, n):
    return jnp.concatenate([jnp.where(row_head == j, x, 0.0) for j in range(n)], axis=1)


def _gdn_kernel(q_ref, k_ref, v_ref, z_ref, gc_ref, gr_ref, bt_ref, gl_ref, gk_ref, s0_ref, nw_ref, y_ref, s_ref,
                *, nst, hs, c, dk, dv):
    n = pl.program_id(2)

    @pl.when(n == 0)
    def _():
        s_ref[...] = s0_ref[...]

    R = hs * c
    shift = int(math.log2(c))
    row = lax.broadcasted_iota(jnp.int32, (R, R), 0)
    col = lax.broadcasted_iota(jnp.int32, (R, R), 1)
    same = jnp.right_shift(row, shift) == jnp.right_shift(col, shift)
    tri = same & (row >= col)
    strict = same & (row > col)
    eye = (row == col).astype(F32)
    row_head = jnp.right_shift(lax.broadcasted_iota(jnp.int32, (R, 1), 0), shift)
    for st in range(nst):
        heads = range(st * hs, (st + 1) * hs)
        q = jnp.concatenate([q_ref[0, :, h * dk:(h + 1) * dk] for h in heads], axis=0)
        k = jnp.concatenate([k_ref[0, :, h * dk:(h + 1) * dk] for h in heads], axis=0)
        v = jnp.concatenate([v_ref[0, :, h * dv:(h + 1) * dv] for h in heads], axis=0)
        z = jnp.concatenate([z_ref[0, :, h * dv:(h + 1) * dv] for h in heads], axis=0)
        gcl = gc_ref[0, st, 0]
        grw = gr_ref[0, st, 0]
        bt = bt_ref[0, st, 0]
        g_end = gl_ref[0, st, 0]
        decay = jnp.where(tri, jnp.exp(jnp.where(tri, gcl - grw, 0.0)), 0.0)
        kb = k * bt
        m = jnp.where(strict, _bdot_nt(kb, k) * decay, 0.0)
        tinv = eye - m
        p = m
        for _ in range(shift - 1):
            p = _dot3(p, p)
            tinv = tinv + _dot3(tinv, p)
        u = _dot3(tinv, v * bt)
        w = _dot3(tinv, kb * jnp.exp(gcl))
        qk = jnp.where(tri, _bdot_nt(q, k) * decay, 0.0)
        S = jnp.concatenate([s_ref[0, h] for h in heads], axis=0)
        v_new = u - _bdot(_head_blocks(w, row_head, hs), S)
        o = _bdot(_head_blocks(q * jnp.exp(gcl), row_head, hs), S) + _bdot(qk, v_new)
        s_new = S * jnp.exp(gk_ref[0, st, 0]) + _bdot_tn(_head_blocks(k * jnp.exp(g_end - gcl), row_head, hs), v_new)
        o = o * lax.rsqrt(jnp.mean(o * o, axis=-1, keepdims=True) + NORM_EPS) * nw_ref[...]
        y = (o * _silu(z)).astype(y_ref.dtype)
        for j, h in enumerate(heads):
            s_ref[0, h] = s_new[j * dk:(j + 1) * dk]
            y_ref[0, :, h * dv:(h + 1) * dv] = y[j * c:(j + 1) * c]


def gdn_core(q, k, v, z, gcs, beta, S0, norm_w, c):
    B, L, _ = q.shape
    H, dk, dv = S0.shape[1:]
    hs = 4
    nst = min(H // hs, 4)
    hb = hs * nst
    nc = L // c
    R = hs * c

    def stacked(t):
        return jnp.transpose(t.reshape(B, nc, c, H // hs, hs), (0, 3, 1, 4, 2)).reshape(B, H // hs, nc, R)

    gt = stacked(gcs)
    g_end = gt.reshape(B, H // hs, nc, hs, c)[..., -1:]
    gcol, grow, bcol = gt[..., None], gt[..., None, :], stacked(beta)[..., None]
    gl = jnp.broadcast_to(g_end, (B, H // hs, nc, hs, c)).reshape(B, H // hs, nc, R, 1)
    gk = jnp.broadcast_to(g_end, (B, H // hs, nc, hs, dk)).reshape(B, H // hs, nc, hs * dk, 1)
    qk_spec = pl.BlockSpec((1, c, hb * dk), lambda b, g, n: (b, n, g))
    v_spec = pl.BlockSpec((1, c, hb * dv), lambda b, g, n: (b, n, g))
    col_spec = pl.BlockSpec((1, nst, 1, R, 1), lambda b, g, n: (b, g, n, 0, 0))
    row_spec = pl.BlockSpec((1, nst, 1, 1, R), lambda b, g, n: (b, g, n, 0, 0))
    gk_spec = pl.BlockSpec((1, nst, 1, hs * dk, 1), lambda b, g, n: (b, g, n, 0, 0))
    s_spec = pl.BlockSpec((1, hb, dk, dv), lambda b, g, n: (b, g, 0, 0))
    return pl.pallas_call(
        functools.partial(_gdn_kernel, nst=nst, hs=hs, c=c, dk=dk, dv=dv),
        grid=(B, H // hb, nc),
        in_specs=[qk_spec, qk_spec, v_spec, v_spec, col_spec, row_spec, col_spec, col_spec, gk_spec, s_spec,
                  pl.BlockSpec((1, dv), lambda b, g, n: (0, 0))],
        out_specs=[v_spec, s_spec],
        out_shape=[jax.ShapeDtypeStruct((B, L, H * dv), BF16), jax.ShapeDtypeStruct((B, H, dk, dv), F32)],
        compiler_params=_params(("parallel", "parallel", "arbitrary")),
    )(q, k, v, z, gcol, grow, bcol, gl, gk, S0, norm_w.reshape(1, dv))


def _ret_kernel(q_ref, k_ref, v_ref, gz_ref, gc_ref, gr_ref, s0_ref, gn_ref, y_ref, s_ref, *, hb, c, dk, dv):
    n = pl.program_id(2)

    @pl.when(n == 0)
    def _():
        s_ref[...] = s0_ref[...]

    row = lax.broadcasted_iota(jnp.int32, (c, c), 0)
    col = lax.broadcasted_iota(jnp.int32, (c, c), 1)
    tri = row >= col
    for h in range(hb):
        qh = q_ref[0, :, h * dk:(h + 1) * dk]
        kh = k_ref[0, :, h * dk:(h + 1) * dk] * (dk ** -0.5)
        vh = v_ref[0, :, h * dv:(h + 1) * dv]
        gcl = gc_ref[0, h]
        grw = gr_ref[0, h, 0]
        decay = jnp.where(tri, jnp.exp(jnp.where(tri, gcl - grw, 0.0)), 0.0)
        S = s_ref[0, h]
        s = _bdot_nt(qh, kh) * decay
        o = _bdot(s, vh) + _bdot(qh * jnp.exp(gcl), S)
        g_last = gcl[c - 1:c, :]
        s_ref[0, h] = jnp.exp(g_last) * S + _bdot_tn(kh * jnp.exp(g_last - gcl), vh)
        mu = jnp.mean(o, axis=-1, keepdims=True)
        oc = o - mu
        var = jnp.mean(oc * oc, axis=-1, keepdims=True)
        on = oc * lax.rsqrt(var + NORM_EPS) * gn_ref[h]
        gz = gz_ref[0, :, h * dv:(h + 1) * dv]
        y_ref[0, :, h * dv:(h + 1) * dv] = (_silu(gz) * on).astype(y_ref.dtype)


def ret_core(proj, gcs, S0, gn_w, c):
    B, L, _ = proj.shape
    H, dk, dv = S0.shape[1:]
    hb = 4
    nc = L // c
    gt = jnp.transpose(gcs, (0, 2, 1))
    gcol = gt[..., None]
    grow = gt.reshape(B, H, nc, 1, c)
    ng = H // hb
    assert (2 * H * dk) % (hb * dv) == 0
    v0 = 2 * H * dk // (hb * dv)
    q_spec = pl.BlockSpec((1, c, hb * dk), lambda b, g, n: (b, n, g))
    k_spec = pl.BlockSpec((1, c, hb * dk), lambda b, g, n: (b, n, ng + g))
    vin_spec = pl.BlockSpec((1, c, hb * dv), lambda b, g, n: (b, n, v0 + g))
    gz_spec = pl.BlockSpec((1, c, hb * dv), lambda b, g, n: (b, n, v0 + ng + g))
    v_spec = pl.BlockSpec((1, c, hb * dv), lambda b, g, n: (b, n, g))
    col_spec = pl.BlockSpec((1, hb, c, 1), lambda b, g, n: (b, g, n, 0))
    row_spec = pl.BlockSpec((1, hb, 1, 1, c), lambda b, g, n: (b, g, n, 0, 0))
    s_spec = pl.BlockSpec((1, hb, dk, dv), lambda b, g, n: (b, g, 0, 0))
    return pl.pallas_call(
        functools.partial(_ret_kernel, hb=hb, c=c, dk=dk, dv=dv),
        grid=(B, H // hb, nc),
        in_specs=[q_spec, k_spec, vin_spec, gz_spec, col_spec, row_spec, s_spec,
                  pl.BlockSpec((hb, 1, dv), lambda b, g, n: (g, 0, 0))],
        out_specs=[v_spec, s_spec],
        out_shape=[jax.ShapeDtypeStruct((B, L, H * dv), BF16), jax.ShapeDtypeStruct((B, H, dk, dv), F32)],
        compiler_params=_params(("parallel", "parallel", "arbitrary")),
    )(proj, proj, proj, proj, gcol, grow, S0, gn_w.reshape(H, 1, dv))


def _lru_kernel(x_ref, gi_ref, wg_ref, bg_ref, sp_ref, h0_ref, y_ref, hl_ref, a_s, b_s, hs_s, h_s, *, tl, seq):
    t = pl.program_id(2)
    x = x_ref[0]
    r = _sigmoid(_bdot(x, wg_ref[0, 0]) + bg_ref[0])
    ig = _sigmoid(_bdot(x, wg_ref[1, 0]) + bg_ref[1])
    log_a = -LRU_C * r * sp_ref[...]
    a = jnp.exp(log_a)
    b = jnp.sqrt(1.0 - jnp.exp(2.0 * log_a)) * (ig * x)
    if seq:
        @pl.when(t == 0)
        def _():
            h_s[...] = h0_ref[0]

        a_s[...] = a
        b_s[...] = b

        def body(i, h):
            h = a_s[pl.ds(i, 1), :] * h + b_s[pl.ds(i, 1), :]
            hs_s[pl.ds(i, 1), :] = h
            return h

        h = lax.fori_loop(0, tl, body, h_s[...])
        h_s[...] = h
        hl_ref[0] = h
        hs = hs_s[...]
    else:
        hs = a * h0_ref[0] + b
        hl_ref[0] = hs
    y_ref[0] = (hs * _gelu_tanh(gi_ref[0])).astype(y_ref.dtype)


def lru_core(xc, gate_in, wg2, bg, sp, h0, seq):
    B, L, C = xc.shape
    sb = wg2.shape[-1]
    ns = C // sb
    tl = min(L, 256)
    x_spec = pl.BlockSpec((1, tl, sb), lambda b, s, t: (b, t, s))
    h_rows = 1 if seq else tl
    h_spec = pl.BlockSpec((1, h_rows, sb), (lambda b, s, t: (b, 0, s)) if seq else (lambda b, s, t: (b, t, s)))
    return pl.pallas_call(
        functools.partial(_lru_kernel, tl=tl, seq=seq),
        grid=(B, ns, L // tl),
        in_specs=[x_spec, x_spec,
                  pl.BlockSpec((2, 1, sb, sb), lambda b, s, t: (0, s, 0, 0)),
                  pl.BlockSpec((2, 1, sb), lambda b, s, t: (0, 0, s)),
                  pl.BlockSpec((1, sb), lambda b, s, t: (0, s)),
                  h_spec],
        out_specs=[x_spec, h_spec],
        out_shape=[jax.ShapeDtypeStruct((B, L, C), BF16), jax.ShapeDtypeStruct((B, h_rows if seq else L, C), F32)],
        scratch_shapes=[pltpu.VMEM((tl, sb), F32), pltpu.VMEM((tl, sb), F32), pltpu.VMEM((tl, sb), F32),
                        pltpu.VMEM((1, sb), F32)],
        compiler_params=_params(("parallel", "parallel", "arbitrary")),
    )(xc, gate_in, wg2, bg, sp, h0)


def _moba_prompt_kernel(q_ref, k_ref, v_ref, sl_ref, o_ref, km_s, *, nb, blk, hd):
    qi = pl.program_id(2)

    @pl.when(qi == 0)
    def _():
        km_s[...] = jnp.zeros_like(km_s)
        for n in range(nb):
            km_s[n:n + 1, :] = jnp.mean(k_ref[0, n * blk:(n + 1) * blk, :], axis=0, keepdims=True)

    q = q_ref[0] * (hd ** -0.5)
    lane = lax.broadcasted_iota(jnp.int32, (blk, LANE), 1)
    gate = jnp.where(lane < qi, _bdot_nt(q, km_s[...]), -jnp.inf)
    sel = jnp.zeros((blk, LANE), F32)
    for _ in range(min(MOBA_TOPK, nb)):
        gmax = jnp.max(gate, axis=-1, keepdims=True)
        imax = jnp.min(jnp.where(gate == gmax, lane, LANE), axis=-1, keepdims=True)
        hit = (lane == imax) & (gmax > -jnp.inf)
        sel = jnp.where(hit, 1.0, sel)
        gate = jnp.where(lane == imax, -jnp.inf, gate)
    slope = sl_ref[0]
    rowi = lax.broadcasted_iota(jnp.int32, (blk, blk), 0)
    coli = lax.broadcasted_iota(jnp.int32, (blk, blk), 1)
    s_blocks = []
    for n in range(nb):
        s = _bdot_nt(q, k_ref[0, n * blk:(n + 1) * blk, :])
        dist = (qi - n) * blk + rowi - coli
        ok = ((sel[:, n:n + 1] > 0.0) | (qi == n)) & (dist >= 0)
        s_blocks.append(jnp.where(ok, s - slope * dist.astype(F32), NEG_BIG))
    m = s_blocks[0].max(axis=-1, keepdims=True)
    for n in range(1, nb):
        m = jnp.maximum(m, s_blocks[n].max(axis=-1, keepdims=True))
    l = jnp.zeros((blk, 1), F32)
    acc = jnp.zeros((blk, hd), F32)
    for n in range(nb):
        p = jnp.where(s_blocks[n] > 0.5 * NEG_BIG, jnp.exp(s_blocks[n] - m), 0.0)
        l = l + p.sum(axis=-1, keepdims=True)
        acc = acc + _bdot(p, v_ref[0, n * blk:(n + 1) * blk, :])
    o_ref[0] = (acc / l).astype(o_ref.dtype)


def moba_prompt(proj, slopes):
    B, L, D3 = proj.shape
    D = D3 // 3
    H = slopes.shape[0]
    hd = D // H
    blk = MOBA_BLOCK
    nb = L // blk
    q_spec = pl.BlockSpec((1, blk, hd), lambda b, h, i: (b, i, h))
    return pl.pallas_call(
        functools.partial(_moba_prompt_kernel, nb=nb, blk=blk, hd=hd),
        grid=(B, H, nb),
        in_specs=[q_spec,
                  pl.BlockSpec((1, L, hd), lambda b, h, i: (b, 0, H + h)),
                  pl.BlockSpec((1, L, hd), lambda b, h, i: (b, 0, 2 * H + h)),
                  pl.BlockSpec((1, 1, 1), lambda b, h, i: (h, 0, 0))],
        out_specs=q_spec,
        out_shape=jax.ShapeDtypeStruct((B, L, D), BF16),
        scratch_shapes=[pltpu.VMEM((LANE, hd), F32)],
        compiler_params=_params(("parallel", "parallel", "arbitrary")),
    )(proj, proj, proj, slopes.reshape(H, 1, 1))


def _kblock_sum_kernel(pt_ref, k_ref, o_ref, *, ppb):
    p = pl.program_id(1)

    @pl.when(p % ppb == 0)
    def _():
        o_ref[...] = jnp.zeros_like(o_ref)

    o_ref[0, 0] += jnp.sum(k_ref[0, 0], axis=0)


def kblock_sums(cache_k, page_table, layer):
    n_seq, n_pages = page_table.shape
    _, _, page, H, hd = cache_k.shape
    ppb = MOBA_BLOCK // page
    grid_spec = pltpu.PrefetchScalarGridSpec(
        num_scalar_prefetch=1, grid=(n_seq, n_pages),
        in_specs=[pl.BlockSpec((1, 1, page, H, hd), lambda b, p, pt: (layer, pt[b, p], 0, 0, 0))],
        out_specs=pl.BlockSpec((1, 1, H, hd), lambda b, p, pt: (b, p // ppb, 0, 0)))
    return pl.pallas_call(
        functools.partial(_kblock_sum_kernel, ppb=ppb), grid_spec=grid_spec,
        out_shape=jax.ShapeDtypeStruct((n_seq, n_pages // ppb, H, hd), F32),
        compiler_params=_params(("parallel", "arbitrary")),
    )(page_table, cache_k)


def _moba_step_kernel(pt_ref, q_ref, kn_ref, vn_ref, ks_ref, sl_ref, k_ref, v_ref, o_ref, m_s, l_s, acc_s, sel_s,
                      *, page, n_pages, hd):
    p = pl.program_id(1)
    q = q_ref[0] * (hd ** -0.5)

    @pl.when(p == 0)
    def _():
        m_s[...] = jnp.sum(q * kn_ref[0], axis=-1, keepdims=True)
        l_s[...] = jnp.ones_like(l_s)
        acc_s[...] = vn_ref[0]
        gate = jnp.sum(ks_ref[0] * (1.0 / MOBA_BLOCK) * q[None], axis=-1, keepdims=True)
        blk = lax.broadcasted_iota(jnp.int32, gate.shape, 0).astype(F32)
        sel = jnp.zeros_like(gate)
        for _ in range(min(MOBA_TOPK, gate.shape[0])):
            gmax = jnp.max(gate, axis=0, keepdims=True)
            imax = jnp.min(jnp.where(gate == gmax, blk, float(2 * gate.shape[0])), axis=0, keepdims=True)
            sel = jnp.where(blk == imax, 1.0, sel)
            gate = jnp.where(blk == imax, -jnp.inf, gate)
        sel_s[...] = sel

    k = k_ref[0, 0]
    s = jnp.sum(k * q[None], axis=-1, keepdims=True)
    tok = lax.broadcasted_iota(jnp.int32, s.shape, 0)
    dist = (n_pages * page - (p * page + tok)).astype(F32)
    ok = sel_s[p // (MOBA_BLOCK // page)][None] > 0.0
    s = jnp.where(ok, s - sl_ref[...][None] * dist, NEG_BIG)
    m_old = m_s[...]
    m_new = jnp.maximum(m_old, jnp.max(s, axis=0))
    alpha = jnp.exp(m_old - m_new)
    pr = jnp.where(ok, jnp.exp(s - m_new[None]), 0.0)
    l_s[...] = alpha * l_s[...] + jnp.sum(pr, axis=0)
    acc_s[...] = alpha * acc_s[...] + jnp.sum(pr * v_ref[0, 0], axis=0)
    m_s[...] = m_new

    @pl.when(p == n_pages - 1)
    def _():
        o_ref[0] = (acc_s[...] / l_s[...]).astype(o_ref.dtype)


def moba_step(q, k_new, v_new, k_sums, slopes, cache_k, cache_v, page_table, layer):
    n_seq, n_pages = page_table.shape
    _, _, page, H, hd = cache_k.shape
    n_blk = k_sums.shape[1]
    assert n_blk * MOBA_BLOCK == n_pages * page
    vec = pl.BlockSpec((1, H, hd), lambda b, p, pt: (b, 0, 0))
    kv = pl.BlockSpec((1, 1, page, H, hd), lambda b, p, pt: (layer, pt[b, p], 0, 0, 0))
    grid_spec = pltpu.PrefetchScalarGridSpec(
        num_scalar_prefetch=1, grid=(n_seq, n_pages),
        in_specs=[vec, vec, vec,
                  pl.BlockSpec((1, n_blk, H, hd), lambda b, p, pt: (b, 0, 0, 0)),
                  pl.BlockSpec((H, 1), lambda b, p, pt: (0, 0)),
                  kv, kv],
        out_specs=vec,
        scratch_shapes=[pltpu.VMEM((H, 1), F32), pltpu.VMEM((H, 1), F32), pltpu.VMEM((H, hd), F32),
                        pltpu.VMEM((n_blk, H, 1), F32)])
    return pl.pallas_call(
        functools.partial(_moba_step_kernel, page=page, n_pages=n_pages, hd=hd), grid_spec=grid_spec,
        out_shape=jax.ShapeDtypeStruct((n_seq, H, hd), BF16),
        compiler_params=_params(("parallel", "arbitrary")),
    )(page_table, q, k_new, v_new, k_sums, slopes.reshape(H, 1), cache_k, cache_v)


def _top_rows(s, pos, k):
    vals, idxs = [], []
    for _ in range(k):
        mx = jnp.max(s, axis=0, keepdims=True)
        ix = jnp.min(jnp.where(s == mx, pos, float(2 * s.shape[0])), axis=0, keepdims=True)
        s = jnp.where(pos == ix, -jnp.inf, s)
        vals.append(mx)
        idxs.append(ix)
    return vals, idxs


def _peer_route_kernel(q_ref, keys_ref, g_ref, r_ref, c_ref, *, dsub, tt):
    key_pos = lax.broadcasted_iota(jnp.int32, (N_KEYS, tt), 0).astype(F32)
    s0 = _bdot_nt(keys_ref[0], q_ref[:, :dsub])
    s1 = _bdot_nt(keys_ref[1], q_ref[:, dsub:])
    v0, i0 = _top_rows(s0, key_pos, P_TOPK)
    v1, i1 = _top_rows(s1, key_pos, P_TOPK)
    top_s1 = jnp.concatenate(v1, axis=0)
    top_i0 = jnp.concatenate(i0, axis=0)
    top_i1 = jnp.concatenate(i1, axis=0)
    cand = jnp.concatenate([v0[a] + top_s1 for a in range(P_TOPK)], axis=0)
    cand_pos = lax.broadcasted_iota(jnp.int32, (P_TOPK * P_TOPK, tt), 0).astype(F32)
    best_s, best_pos = _top_rows(cand, cand_pos, P_TOPK)
    slot = lax.broadcasted_iota(jnp.int32, (P_TOPK, tt), 0).astype(F32)
    e = [jnp.exp(b - best_s[0]) for b in best_s]
    z = e[0]
    for x in e[1:]:
        z = z + x
    for r in range(P_TOPK):
        a = jnp.floor(best_pos[r] * (1.0 / P_TOPK))
        b = best_pos[r] - a * P_TOPK
        g_ref[r:r + 1, :] = e[r] / z
        r_ref[r:r + 1, :] = jnp.sum(jnp.where(slot == a, top_i0, 0.0), axis=0, keepdims=True)
        c_ref[r:r + 1, :] = jnp.sum(jnp.where(slot == b, top_i1, 0.0), axis=0, keepdims=True)


def peer_route(q, keys):
    T = q.shape[0]
    dsub = keys.shape[-1]
    n_heads = keys.shape[0] // 2
    tt = LANE
    out_spec = pl.BlockSpec((P_TOPK, tt), lambda i, h: (h, i))
    shape = jax.ShapeDtypeStruct((n_heads * P_TOPK, T), F32)
    return pl.pallas_call(
        functools.partial(_peer_route_kernel, dsub=dsub, tt=tt),
        grid=(T // tt, n_heads),
        in_specs=[pl.BlockSpec((tt, 2 * dsub), lambda i, h: (i, h)),
                  pl.BlockSpec((2, N_KEYS, dsub), lambda i, h: (h, 0, 0))],
        out_specs=[out_spec, out_spec, out_spec], out_shape=[shape, shape, shape],
        compiler_params=_params(("parallel", "parallel")),
    )(q, keys)


def _peer_gates_kernel(g_ref, i_ref, j_ref, o_ref, m_s, *, tb):
    iota = lax.broadcasted_iota(jnp.int32, (tb, N_KEYS, LANE), 1).astype(F32)
    a = jnp.where(i_ref[...] == iota, g_ref[...], 0.0)
    a_hi = a.astype(BF16)
    a_lo = (a - a_hi.astype(F32)).astype(BF16)
    bt = jnp.where(j_ref[...] == iota, 1.0, 0.0).astype(BF16)
    dn = (((2,), (2,)), ((0,), (0,)))
    m_s[...] = (lax.dot_general(a_hi, bt, dn, preferred_element_type=F32)
                + lax.dot_general(a_lo, bt, dn, preferred_element_type=F32))
    for i in range(N_KEYS):
        o_ref[:, i * N_KEYS:(i + 1) * N_KEYS] = m_s[:, i, :]


def peer_gate_matrix(gates, rows, cols):
    T, slots = gates.shape
    assert slots == LANE
    tb = min(T, 32)
    spec = pl.BlockSpec((tb, 1, slots), lambda i: (i, 0, 0))
    return pl.pallas_call(
        functools.partial(_peer_gates_kernel, tb=tb),
        grid=(T // tb,), in_specs=[spec, spec, spec],
        out_specs=pl.BlockSpec((tb, N_KEYS * N_KEYS), lambda i: (i, 0)),
        out_shape=jax.ShapeDtypeStruct((T, N_KEYS * N_KEYS), F32),
        scratch_shapes=[pltpu.VMEM((tb, N_KEYS, N_KEYS), F32)],
        compiler_params=_params(("parallel",)),
    )(gates.reshape(T, 1, slots), rows.reshape(T, 1, slots), cols.reshape(T, 1, slots))


def _peer_experts_kernel(x_ref, g_ref, u_ref, v_ref, o_ref):
    e = pl.program_id(1)

    @pl.when(e == 0)
    def _():
        o_ref[...] = jnp.zeros_like(o_ref)

    act = _gelu_tanh(_bdot_nt(x_ref[...], u_ref[...]))
    o_ref[...] += _bdot(g_ref[...] * act, v_ref[...])


def peer_experts(x, gmat, u, v):
    T, D = x.shape
    E = u.shape[0]
    tm = min(T, 512)
    te = 512
    return pl.pallas_call(
        _peer_experts_kernel,
        grid=(T // tm, E // te),
        in_specs=[pl.BlockSpec((tm, D), lambda i, e: (i, 0)), pl.BlockSpec((tm, te), lambda i, e: (i, e)),
                  pl.BlockSpec((te, D), lambda i, e: (e, 0)), pl.BlockSpec((te, D), lambda i, e: (e, 0))],
        out_specs=pl.BlockSpec((tm, D), lambda i, e: (i, 0)),
        out_shape=jax.ShapeDtypeStruct((T, D), F32),
        compiler_params=_params(("parallel", "arbitrary")),
    )(x, gmat, u, v)


def peer_ffn(hs, wq, keys, u, v):
    n_heads = keys.shape[0]
    dsub = keys.shape[-1]
    outs = []
    for h, q in zip(hs, proj_pair(hs, wq)):
        B, L, D = h.shape
        T = B * L
        q = jnp.pad(q.reshape(T, -1), ((0, -T % LANE), (0, 0)))
        gates, rows, cols = (t[:, :T].T for t in peer_route(q, keys.reshape(n_heads * 2, N_KEYS, dsub)))
        outs.append(peer_experts(h.reshape(T, D), peer_gate_matrix(gates, rows, cols), u, v).reshape(B, L, D))
    return outs


def causal_dwconv(x, buf, w, b=None):
    L = x.shape[1]
    xp = jnp.concatenate([buf.astype(x.dtype), x], axis=1)
    y = xp[:, 0:L] * w[0]
    for j in range(1, CONV_W):
        y = y + xp[:, j:j + L] * w[j]
    if b is not None:
        y = y + b
    return y, xp[:, L:]


def _pad_rows(t, rows):
    return jnp.pad(t, ((0, 0), (0, rows - t.shape[1])) + ((0, 0),) * (t.ndim - 2))


def _chunk_cumsum(g, c):
    B, L, H = g.shape
    return jnp.cumsum(g.reshape(B, L // c, c, H), axis=2).reshape(B, L, H)


def gdn_mixer(hs, states, params):
    w_in, w_out = params[0], params[-1]
    H, dk, dv = states[0][0].shape[1:]
    n_main = 2 * H * dk + 2 * H * dv
    projs = proj_pair(hs, w_in, n_cols=n_main)
    bas = proj_pair(hs, w_in[:, n_main:])
    ys, new_states = zip(*(_gdn_group(projs[g], bas[g], states[g], params) for g in range(2)))
    return proj_pair(ys, w_out), new_states


def _gdn_group(proj, ba, state, params):
    S0, conv0 = state
    _, conv_w, a_log, dt_bias, norm_w, _ = params
    B, L, _ = proj.shape
    H, dk, dv = S0.shape[1:]
    d_qk, d_v = H * dk, H * dv
    qkv, conv_new = causal_dwconv(proj[..., :2 * d_qk + d_v], conv0, conv_w)
    z = proj[..., 2 * d_qk + d_v:]
    qkv = jax.nn.silu(qkv)
    q = qkv[..., :d_qk].reshape(B, L, H, dk)
    k = qkv[..., d_qk:2 * d_qk].reshape(B, L, H, dk)
    v = qkv[..., 2 * d_qk:]
    q = (q * lax.rsqrt(jnp.sum(q * q, -1, keepdims=True) + NORM_EPS) * (dk ** -0.5)).reshape(B, L, d_qk)
    k = (k * lax.rsqrt(jnp.sum(k * k, -1, keepdims=True) + NORM_EPS)).reshape(B, L, d_qk)
    beta = jax.nn.sigmoid(ba[..., :H])
    g = -jnp.exp(a_log) * jax.nn.softplus(ba[..., H:] + dt_bias)
    c = math.gcd(L, GDN_CHUNK)
    if c < SUBLANE:
        Lp = SUBLANE
        q, k, v, z, g, beta = (_pad_rows(t, Lp) for t in (q, k, v, z, g, beta))
        c = Lp
    y, S = gdn_core(q, k, v, z, _chunk_cumsum(g, c), beta, S0, norm_w, c)
    return y[:, :L], (S, conv_new)


def lru_mixer(hs, states, params):
    w_in, w_out = params[0], params[-1]
    projs = proj_pair(hs, w_in)
    ys, new_states = zip(*(_lru_group(projs[g], states[g], params) for g in range(2)))
    return proj_pair(ys, w_out), new_states


def _lru_group(proj, state, params):
    h0, conv0 = state
    _, conv_w, conv_b, w_gates, b_gates, lam, _ = params
    B, L, _ = proj.shape
    C = h0.shape[-1]
    xc, conv_new = causal_dwconv(proj[..., C:], conv0, conv_w, conv_b)
    nb, bw = w_gates.shape[1], w_gates.shape[2]
    per = (bw * LANE // math.gcd(bw, LANE)) // bw
    wg = w_gates.reshape(2, nb // per, per, bw, bw)
    eye = jnp.eye(per, dtype=w_gates.dtype)
    wg2 = jnp.einsum('gspij,pq->gspiqj', wg, eye).reshape(2, nb // per, per * bw, per * bw)
    sp = jax.nn.softplus(-lam).reshape(1, C)
    bg = b_gates.reshape(2, 1, C)
    if L > 1:
        y, hl = lru_core(xc, proj, wg2, bg, sp, h0.reshape(B, 1, C), True)
        h_last = hl.reshape(B, C)
    else:
        y, hl = lru_core(xc.reshape(1, B, C), proj.reshape(1, B, 2 * C), wg2, bg, sp, h0.reshape(1, B, C), False)
        y = y.reshape(B, 1, C)
        h_last = hl.reshape(B, C)
    return y, (h_last, conv_new)


def ret_mixer(hs, states, params):
    w_in, gn_w, w_out = params
    projs = proj_pair(hs, w_in)
    ys, new_states = zip(*(_ret_group(projs[g], states[g], gn_w) for g in range(2)))
    return proj_pair(ys, w_out), new_states


def _ret_group(proj, state, gn_w):
    (S0,) = state
    B, L, _ = proj.shape
    H = S0.shape[1]
    log_gamma = jnp.log1p(-jnp.exp2(-5.0 - jnp.arange(H, dtype=F32)))
    c = math.gcd(L, RET_CHUNK)
    Lp = L
    if c < SUBLANE:
        Lp = c = SUBLANE
        proj = _pad_rows(proj, Lp)
    pos = jnp.arange(Lp)
    steps = jnp.minimum(pos % c + 1, jnp.maximum(L - (pos // c) * c, 0)).astype(F32)
    gcs = jnp.broadcast_to(steps[None, :, None] * log_gamma[None, None, :], (B, Lp, H))
    y, S = ret_core(proj, gcs, S0, gn_w, c)
    return y[:, :L], (S,)


def moba_mixer(hs, pasts, params, n_heads):
    w_in, w_out = params
    projs = proj_pair(hs, w_in)
    ys, new_states = zip(*(_moba_group(projs[g], pasts[g], n_heads) for g in range(2)))
    return proj_pair(ys, w_out), new_states


def _moba_group(proj, past, n_heads):
    B, L, D3 = proj.shape
    D = D3 // 3
    hd = D // n_heads
    q, k, v = proj[..., :D], proj[..., D:2 * D], proj[..., 2 * D:]
    slopes = jnp.exp2(-8.0 * (jnp.arange(n_heads, dtype=F32) + 1.0) / n_heads)
    if past is None:
        o = moba_prompt(proj, slopes)
    else:
        cache_k, cache_v, page_table, layer = past
        k_sums = kblock_sums(cache_k, page_table, layer)
        o = moba_step(q.reshape(B, n_heads, hd), k.reshape(B, n_heads, hd), v.reshape(B, n_heads, hd), k_sums,
                      slopes, cache_k, cache_v, page_table, layer).reshape(B, L, D)
    return o, (k.reshape(B, L, n_heads, hd), v.reshape(B, L, n_heads, hd))


def kernel(x_prompt, x_sample, c_prompt, c_sample, state_gdn_S, state_gdn_conv, state_lru_h, state_lru_conv, state_ret_S, cache_k, cache_v, page_table, w_ada, b_ada, ln_g, ln_b, peer_wq, peer_keys, peer_u, peer_v, gdn_w_in, gdn_conv_w, gdn_a_log, gdn_dt_bias, gdn_norm_w, gdn_w_out, lru_w_in, lru_conv_w, lru_conv_b, lru_w_gates, lru_b_gates, lru_lambda, lru_w_out, ret_w_in, ret_gn_w, ret_w_out, moba_w_in, moba_w_out):
    depth, D = w_ada.shape[0], w_ada.shape[1]
    n_mixers = 4
    Bp, Bs = x_prompt.shape[0], x_sample.shape[0]
    H_A, DK_A, DV_A = state_gdn_S.shape[2:]
    conv_dim_a = state_gdn_conv.shape[-1]
    C_B = state_lru_h.shape[-1]
    H_C, DK_C, DV_C = state_ret_S.shape[2:]
    H_D = cache_k.shape[3]

    c_all = jax.nn.silu(jnp.concatenate([c_prompt, c_sample], axis=0))
    n_c = c_all.shape[0]
    c_pad = jnp.pad(c_all, ((0, -n_c % SUBLANE), (0, 0)))
    xs = [x_prompt, x_sample]
    groups = ((0, Bp), (Bp, Bs))
    new_states = {(m, grp): [] for m in range(n_mixers) for grp in range(2)}
    hs = [None, None]
    mod_all = [matmul(c_pad, w_ada[i], bias=b_ada[i]) for i in range(depth)]
    mods = [[[mod[s:s + n, None, r * D:(r + 1) * D] for r in range(6)] for s, n in groups] for mod in mod_all]
    for i in range(depth):
        m, j = i % n_mixers, i // n_mixers
        peer_params = (peer_wq[i], peer_keys[i], peer_u[i].astype(BF16), peer_v[i].astype(BF16))
        if i == 0:
            hs = [modulate(xs[g], mods[0][g][1], mods[0][g][0]) for g in range(2)]
        if m == 0:
            params = (gdn_w_in[j], gdn_conv_w[j], gdn_a_log[j], gdn_dt_bias[j], gdn_norm_w[j], gdn_w_out[j])
            states = ((jnp.zeros((Bp, H_A, DK_A, DV_A), F32), jnp.zeros((Bp, CONV_W - 1, conv_dim_a), F32)),
                      (state_gdn_S[j], state_gdn_conv[j]))
            ys, ns = gdn_mixer(hs, states, params)
        elif m == 1:
            params = (lru_w_in[j], lru_conv_w[j], lru_conv_b[j], lru_w_gates[j], lru_b_gates[j], lru_lambda[j],
                      lru_w_out[j])
            states = ((jnp.zeros((Bp, C_B), F32), jnp.zeros((Bp, CONV_W - 1, C_B), F32)),
                      (state_lru_h[j], state_lru_conv[j]))
            ys, ns = lru_mixer(hs, states, params)
        elif m == 2:
            states = ((jnp.zeros((Bp, H_C, DK_C, DV_C), F32),), (state_ret_S[j],))
            ys, ns = ret_mixer(hs, states, (ret_w_in[j], ret_gn_w[j], ret_w_out[j]))
        else:
            ys, ns = moba_mixer(hs, (None, (cache_k, cache_v, page_table, j)), (moba_w_in[j], moba_w_out[j]), H_D)
        x1, h2 = [], []
        for g in range(2):
            new_states[(m, g)].append(ns[g])
            _, _, g1, sh2, sc2, _ = mods[i][g]
            xg, hg = ln_residual(xs[g], ys[g], g1, sc2, sh2, ln_g[i, 0], ln_b[i, 0])
            x1.append(xg)
            h2.append(hg)
        y2 = peer_ffn(h2, *peer_params)
        for g in range(2):
            if i + 1 < depth:
                sh_n, sc_n = mods[i + 1][g][0], mods[i + 1][g][1]
            else:
                sh_n = sc_n = jnp.zeros((xs[g].shape[0], 1, D), F32)
            xs[g], hs[g] = ln_residual(x1[g], y2[g], mods[i][g][5], sc_n, sh_n, ln_g[i, 1], ln_b[i, 1])

    def stacked(m, grp):
        return [jnp.stack(parts) for parts in zip(*new_states[(m, grp)])]

    gdn_S_p, gdn_conv_p = stacked(0, 0)
    gdn_S_s, gdn_conv_s = stacked(0, 1)
    lru_h_p, lru_conv_p = stacked(1, 0)
    lru_h_s, lru_conv_s = stacked(1, 1)
    (ret_S_p,) = stacked(2, 0)
    (ret_S_s,) = stacked(2, 1)
    k_p, v_p = stacked(3, 0)
    k_s, v_s = stacked(3, 1)
    return (xs[0], xs[1], gdn_S_p, gdn_S_s, gdn_conv_p, gdn_conv_s, lru_h_p, lru_h_s, lru_conv_p, lru_conv_s,
            ret_S_p, ret_S_s, k_p, k_s, v_p, v_s)
```

```python
import functools
import math

import jax
import jax.numpy as jnp
from jax import lax
from jax.experimental import pallas as pl
from jax.experimental.pallas import tpu as pltpu

F32 = jnp.float32
BF16 = jnp.bfloat16

DEPTH = 4
ALPHA = (2.0 * DEPTH) ** 0.25
CONV_W = 4
LN_EPS = 1e-5
NORM_EPS = 1e-6
GDN_CHUNK = 64
RET_CHUNK = 64
LRU_C = 8.0
MOBA_BLOCK = 256
MOBA_TOPK = 3
P_TOPK = 16
N_KEYS = 128
LANE = 128
SUBLANE = 8
VMEM_LIMIT = 56 * 1024 * 1024
NEG_BIG = -1e30

_NT = (((1,), (1,)), ((), ()))
_TN = (((0,), (0,)), ((), ()))


def _tile(n, cap, align):
    for t in range(min(cap, n) // align * align, 0, -align):
        if n % t == 0:
            return t
    return n


def _params(sem):
    return pltpu.CompilerParams(dimension_semantics=sem, vmem_limit_bytes=VMEM_LIMIT)


def _bdot(a, b):
    return jnp.dot(a.astype(BF16), b.astype(BF16), preferred_element_type=F32)


def _bdot_nt(a, b):
    return lax.dot_general(a.astype(BF16), b.astype(BF16), _NT, preferred_element_type=F32)


def _bdot_tn(a, b):
    return lax.dot_general(a.astype(BF16), b.astype(BF16), _TN, preferred_element_type=F32)


def _fdot(a, b):
    return jnp.dot(a, b, preferred_element_type=F32, precision=lax.Precision.HIGHEST)


def _gelu_tanh(x):
    return 0.5 * x * (1.0 + jnp.tanh(math.sqrt(2.0 / math.pi) * (x + 0.044715 * (x * x * x))))


def _sigmoid(x):
    return 1.0 / (1.0 + jnp.exp(-x))


def _silu(x):
    return x * _sigmoid(x)


def _mm_kernel(a_ref, b_ref, *rest, nk, has_bias):
    if has_bias:
        bias_ref, o_ref, acc_ref = rest
    else:
        o_ref, acc_ref = rest
    k = pl.program_id(2)

    @pl.when(k == 0)
    def _():
        acc_ref[...] = jnp.zeros_like(acc_ref)

    acc_ref[...] += _bdot(a_ref[...], b_ref[...])

    @pl.when(k == nk - 1)
    def _():
        r = acc_ref[...]
        if has_bias:
            r = r + bias_ref[...]
        o_ref[...] = r.astype(o_ref.dtype)


def matmul(a, b, *, n_cols=None, bias=None, out_dtype=F32):
    M, K = a.shape
    N = b.shape[1] if n_cols is None else n_cols
    tm = _tile(M, 1024, SUBLANE)
    tn = _tile(N, 1024 if M > 64 else 2048, LANE)
    tk = _tile(K, 512, LANE)
    nk = K // tk
    in_specs = [pl.BlockSpec((tm, tk), lambda i, j, k: (i, k)),
                pl.BlockSpec((tk, tn), lambda i, j, k: (k, j))]
    args = [a, b]
    if bias is not None:
        in_specs.append(pl.BlockSpec((1, tn), lambda i, j, k: (0, j)))
        args.append(bias.reshape(1, -1))
    return pl.pallas_call(
        functools.partial(_mm_kernel, nk=nk, has_bias=bias is not None),
        grid=(M // tm, N // tn, nk),
        in_specs=in_specs,
        out_specs=pl.BlockSpec((tm, tn), lambda i, j, k: (i, j)),
        out_shape=jax.ShapeDtypeStruct((M, N), out_dtype),
        scratch_shapes=[pltpu.VMEM((tm, tn), F32)],
        compiler_params=_params(("parallel", "parallel", "arbitrary")),
    )(*args)


def _mm_pair_kernel(ap_ref, as_ref, b_ref, op_ref, os_ref, b_s):
    @pl.when(pl.program_id(1) == 0)
    def _():
        b_s[...] = b_ref[...].astype(BF16)
        os_ref[...] = jnp.dot(as_ref[...].astype(BF16), b_s[...], preferred_element_type=F32)

    op_ref[...] = jnp.dot(ap_ref[...].astype(BF16), b_s[...], preferred_element_type=F32)


def matmul_pair(a_p, a_s, b, n_cols=None):
    Mp, K = a_p.shape
    Ms = a_s.shape[0]
    N = b.shape[1] if n_cols is None else n_cols
    wide = K <= 5120
    tm = _tile(Mp, 1024 if wide else 512, SUBLANE)
    tn = _tile(N, 512 if wide else 256, LANE)
    return pl.pallas_call(
        _mm_pair_kernel,
        grid=(N // tn, Mp // tm),
        in_specs=[pl.BlockSpec((tm, K), lambda j, i: (i, 0)), pl.BlockSpec((Ms, K), lambda j, i: (0, 0)),
                  pl.BlockSpec((K, tn), lambda j, i: (0, j))],
        out_specs=[pl.BlockSpec((tm, tn), lambda j, i: (i, j)), pl.BlockSpec((Ms, tn), lambda j, i: (0, j))],
        out_shape=[jax.ShapeDtypeStruct((Mp, N), F32), jax.ShapeDtypeStruct((Ms, N), F32)],
        scratch_shapes=[pltpu.VMEM((K, tn), BF16)],
        compiler_params=_params(("parallel", "arbitrary")),
    )(a_p, a_s, b)


def proj_pair(xs, w, n_cols=None):
    (Bp, Lp, K), (Bs, Ls, _) = xs[0].shape, xs[1].shape
    o_p, o_s = matmul_pair(xs[0].reshape(Bp * Lp, K), xs[1].reshape(Bs * Ls, K), w, n_cols)
    return o_p.reshape(Bp, Lp, -1), o_s.reshape(Bs, Ls, -1)


def _modulate_kernel(x_ref, sc_ref, sh_ref, h_ref):
    h_ref[0] = (x_ref[0] * (1.0 + sc_ref[0]) + sh_ref[0]).astype(h_ref.dtype)


def modulate(x, sc, sh):
    B, L, D = x.shape
    tl = min(L, 256)
    row = pl.BlockSpec((1, tl, D), lambda b, t: (b, t, 0))
    vec = pl.BlockSpec((1, 1, D), lambda b, t: (b, 0, 0))
    return pl.pallas_call(
        _modulate_kernel, grid=(B, L // tl), in_specs=[row, vec, vec], out_specs=row,
        out_shape=jax.ShapeDtypeStruct((B, L, D), BF16),
        compiler_params=_params(("parallel", "parallel")),
    )(x, sc, sh)


def _ln_kernel(x_ref, y_ref, g_ref, sc_ref, sh_ref, lg_ref, lb_ref, xo_ref, ho_ref):
    z = ALPHA * x_ref[0] + (1.0 + g_ref[0]) * y_ref[0]
    mu = jnp.mean(z, axis=-1, keepdims=True)
    zc = z - mu
    var = jnp.mean(zc * zc, axis=-1, keepdims=True)
    xn = zc * lax.rsqrt(var + LN_EPS) * lg_ref[...] + lb_ref[...]
    xo_ref[0] = xn
    ho_ref[0] = (xn * (1.0 + sc_ref[0]) + sh_ref[0]).astype(ho_ref.dtype)


def ln_residual(x, y, gate, sc_next, sh_next, ln_g, ln_b):
    B, L, D = x.shape
    tl = min(L, 256)
    row = pl.BlockSpec((1, tl, D), lambda b, t: (b, t, 0))
    vec = pl.BlockSpec((1, 1, D), lambda b, t: (b, 0, 0))
    par = pl.BlockSpec((1, D), lambda b, t: (0, 0))
    return pl.pallas_call(
        _ln_kernel, grid=(B, L // tl),
        in_specs=[row, row, vec, vec, vec, par, par], out_specs=[row, row],
        out_shape=[jax.ShapeDtypeStruct((B, L, D), F32), jax.ShapeDtypeStruct((B, L, D), BF16)],
        compiler_params=_params(("parallel", "parallel")),
    )(x, y, gate, sc_next, sh_next, ln_g.reshape(1, D), ln_b.reshape(1, D))


def _split_bf16(a):
    hi = a.astype(BF16)
    return hi, (a - hi.astype(F32)).astype(BF16)


def _dot3(a, b):
    ah, al = _split_bf16(a)
    bh, bl = _split_bf16(b)
    dot = functools.partial(jnp.dot, preferred_element_type=F32)
    return dot(ah, bh) + (dot(ah, bl) + dot(al, bh))


def _head_blocks(x, row_head, n):
    return jnp.concatenate([jnp.where(row_head == j, x, 0.0) for j in range(n)], axis=1)


def _gdn_kernel(q_ref, k_ref, v_ref, z_ref, gc_ref, gr_ref, bt_ref, gl_ref, gk_ref, s0_ref, nw_ref, y_ref, s_ref,
                *, nst, hs, c, dk, dv):
    n = pl.program_id(2)

    @pl.when(n == 0)
    def _():
        s_ref[...] = s0_ref[...]

    R = hs * c
    shift = int(math.log2(c))
    row = lax.broadcasted_iota(jnp.int32, (R, R), 0)
    col = lax.broadcasted_iota(jnp.int32, (R, R), 1)
    same = jnp.right_shift(row, shift) == jnp.right_shift(col, shift)
    tri = same & (row >= col)
    strict = same & (row > col)
    eye = (row == col).astype(F32)
    row_head = jnp.right_shift(lax.broadcasted_iota(jnp.int32, (R, 1), 0), shift)
    for st in range(nst):
        heads = range(st * hs, (st + 1) * hs)
        q = jnp.concatenate([q_ref[0, :, h * dk:(h + 1) * dk] for h in heads], axis=0)
        k = jnp.concatenate([k_ref[0, :, h * dk:(h + 1) * dk] for h in heads], axis=0)
        v = jnp.concatenate([v_ref[0, :, h * dv:(h + 1) * dv] for h in heads], axis=0)
        z = jnp.concatenate([z_ref[0, :, h * dv:(h + 1) * dv] for h in heads], axis=0)
        gcl = gc_ref[0, st, 0]
        grw = gr_ref[0, st, 0]
        bt = bt_ref[0, st, 0]
        g_end = gl_ref[0, st, 0]
        decay = jnp.where(tri, jnp.exp(jnp.where(tri, gcl - grw, 0.0)), 0.0)
        kb = k * bt
        m = jnp.where(strict, _bdot_nt(kb, k) * decay, 0.0)
        tinv = eye - m
        p = m
        for _ in range(shift - 1):
            p = _dot3(p, p)
            tinv = tinv + _dot3(tinv, p)
        u = _dot3(tinv, v * bt)
        w = _dot3(tinv, kb * jnp.exp(gcl))
        qk = jnp.where(tri, _bdot_nt(q, k) * decay, 0.0)
        S = jnp.concatenate([s_ref[0, h] for h in heads], axis=0)
        v_new = u - _bdot(_head_blocks(w, row_head, hs), S)
        o = _bdot(_head_blocks(q * jnp.exp(gcl), row_head, hs), S) + _bdot(qk, v_new)
        s_new = S * jnp.exp(gk_ref[0, st, 0]) + _bdot_tn(_head_blocks(k * jnp.exp(g_end - gcl), row_head, hs), v_new)
        o = o * lax.rsqrt(jnp.mean(o * o, axis=-1, keepdims=True) + NORM_EPS) * nw_ref[...]
        y = (o * _silu(z)).astype(y_ref.dtype)
        for j, h in enumerate(heads):
            s_ref[0, h] = s_new[j * dk:(j + 1) * dk]
            y_ref[0, :, h * dv:(h + 1) * dv] = y[j * c:(j + 1) * c]


def _dwconv_kernel(x_ref, prev_ref, c0_ref, w_ref, b_ref, o_ref, xs, *, tl, gdn, dk, nq):
    t, j = pl.program_id(1), pl.program_id(2)
    xs[0:SUBLANE, :] = jnp.where(t == 0, c0_ref[0], prev_ref[0])
    xs[SUBLANE:SUBLANE + tl, :] = x_ref[0]
    first = SUBLANE - (CONV_W - 1)
    y = xs[first:first + tl, :] * w_ref[0:1, :]
    for r in range(1, CONV_W):
        y = y + xs[first + r:first + r + tl, :] * w_ref[r:r + 1, :]
    if not gdn:
        o_ref[0] = y + b_ref[...]
        return
    y = _silu(y)
    q_scale = jnp.where(j < nq, dk ** -0.5, 1.0)
    for h in range(y.shape[1] // dk):
        seg = y[:, h * dk:(h + 1) * dk]
        inv = lax.rsqrt(jnp.sum(seg * seg, axis=-1, keepdims=True) + NORM_EPS)
        o_ref[0, :, h * dk:(h + 1) * dk] = jnp.where(j < 2 * nq, seg * inv * q_scale, seg)


def dwconv(proj, col0, conv0, conv_w, bias=None, gdn_heads=None):
    B, L, _ = proj.shape
    conv_dim = conv0.shape[-1]
    gdn = gdn_heads is not None
    H, dk = gdn_heads if gdn else (1, LANE)
    cw = _tile(H * dk if gdn else conv_dim, 1024, dk)
    assert col0 % cw == 0 and conv_dim % cw == 0
    tl = _tile(L, 256, SUBLANE)
    c0 = jnp.pad(conv0, ((0, 0), (SUBLANE - (CONV_W - 1), 0), (0, 0)))
    rpb, j0 = tl // SUBLANE, col0 // cw
    if bias is None:
        bias = jnp.zeros((conv_dim,), F32)
    return pl.pallas_call(
        functools.partial(_dwconv_kernel, tl=tl, gdn=gdn, dk=dk, nq=H * dk // cw),
        grid=(B, L // tl, conv_dim // cw),
        in_specs=[pl.BlockSpec((1, tl, cw), lambda b, t, j: (b, t, j0 + j)),
                  pl.BlockSpec((1, SUBLANE, cw), lambda b, t, j: (b, jnp.maximum(t * rpb - 1, 0), j0 + j)),
                  pl.BlockSpec((1, SUBLANE, cw), lambda b, t, j: (b, 0, j)),
                  pl.BlockSpec((CONV_W, cw), lambda b, t, j: (0, j)),
                  pl.BlockSpec((1, cw), lambda b, t, j: (0, j))],
        out_specs=pl.BlockSpec((1, tl, cw), lambda b, t, j: (b, t, j)),
        out_shape=jax.ShapeDtypeStruct((B, L, conv_dim), F32),
        scratch_shapes=[pltpu.VMEM((SUBLANE + tl, cw), F32)],
        compiler_params=_params(("parallel", "parallel", "parallel")),
    )(proj, proj, c0, conv_w, bias.reshape(1, conv_dim))


def gdn_core(qkv, proj, gcs, beta, S0, norm_w, c):
    B, L, _ = qkv.shape
    H, dk, dv = S0.shape[1:]
    hs = 4
    nst = min(H // hs, 4)
    hb = hs * nst
    nc = L // c
    R = hs * c

    def stacked(t):
        return jnp.transpose(t.reshape(B, nc, c, H // hs, hs), (0, 3, 1, 4, 2)).reshape(B, H // hs, nc, R)

    gt = stacked(gcs)
    g_end = gt.reshape(B, H // hs, nc, hs, c)[..., -1:]
    gcol, grow, bcol = gt[..., None], gt[..., None, :], stacked(beta)[..., None]
    gl = jnp.broadcast_to(g_end, (B, H // hs, nc, hs, c)).reshape(B, H // hs, nc, R, 1)
    gk = jnp.broadcast_to(g_end, (B, H // hs, nc, hs, dk)).reshape(B, H // hs, nc, hs * dk, 1)
    ng = H // hb
    assert (2 * H * dk) % (hb * dv) == 0
    v0 = 2 * H * dk // (hb * dv)
    q_spec = pl.BlockSpec((1, c, hb * dk), lambda b, g, n: (b, n, g))
    k_spec = pl.BlockSpec((1, c, hb * dk), lambda b, g, n: (b, n, ng + g))
    vin_spec = pl.BlockSpec((1, c, hb * dv), lambda b, g, n: (b, n, v0 + g))
    z_spec = pl.BlockSpec((1, c, hb * dv), lambda b, g, n: (b, n, v0 + ng + g))
    y_spec = pl.BlockSpec((1, c, hb * dv), lambda b, g, n: (b, n, g))
    col_spec = pl.BlockSpec((1, nst, 1, R, 1), lambda b, g, n: (b, g, n, 0, 0))
    row_spec = pl.BlockSpec((1, nst, 1, 1, R), lambda b, g, n: (b, g, n, 0, 0))
    gk_spec = pl.BlockSpec((1, nst, 1, hs * dk, 1), lambda b, g, n: (b, g, n, 0, 0))
    s_spec = pl.BlockSpec((1, hb, dk, dv), lambda b, g, n: (b, g, 0, 0))
    return pl.pallas_call(
        functools.partial(_gdn_kernel, nst=nst, hs=hs, c=c, dk=dk, dv=dv),
        grid=(B, ng, nc),
        in_specs=[q_spec, k_spec, vin_spec, z_spec, col_spec, row_spec, col_spec, col_spec, gk_spec, s_spec,
                  pl.BlockSpec((1, dv), lambda b, g, n: (0, 0))],
        out_specs=[y_spec, s_spec],
        out_shape=[jax.ShapeDtypeStruct((B, L, H * dv), BF16), jax.ShapeDtypeStruct((B, H, dk, dv), F32)],
        compiler_params=_params(("parallel", "parallel", "arbitrary")),
    )(qkv, qkv, qkv, proj, gcol, grow, bcol, gl, gk, S0, norm_w.reshape(1, dv))


def _ret_kernel(q_ref, k_ref, v_ref, gz_ref, gc_ref, gr_ref, s0_ref, gn_ref, y_ref, s_ref, *, hb, c, dk, dv):
    n = pl.program_id(2)

    @pl.when(n == 0)
    def _():
        s_ref[...] = s0_ref[...]

    row = lax.broadcasted_iota(jnp.int32, (c, c), 0)
    col = lax.broadcasted_iota(jnp.int32, (c, c), 1)
    tri = row >= col
    for h in range(hb):
        qh = q_ref[0, :, h * dk:(h + 1) * dk]
        kh = k_ref[0, :, h * dk:(h + 1) * dk] * (dk ** -0.5)
        vh = v_ref[0, :, h * dv:(h + 1) * dv]
        gcl = gc_ref[0, h]
        grw = gr_ref[0, h, 0]
        decay = jnp.where(tri, jnp.exp(jnp.where(tri, gcl - grw, 0.0)), 0.0)
        S = s_ref[0, h]
        s = _bdot_nt(qh, kh) * decay
        o = _bdot(s, vh) + _bdot(qh * jnp.exp(gcl), S)
        g_last = gcl[c - 1:c, :]
        s_ref[0, h] = jnp.exp(g_last) * S + _bdot_tn(kh * jnp.exp(g_last - gcl), vh)
        mu = jnp.mean(o, axis=-1, keepdims=True)
        oc = o - mu
        var = jnp.mean(oc * oc, axis=-1, keepdims=True)
        on = oc * lax.rsqrt(var + NORM_EPS) * gn_ref[h]
        gz = gz_ref[0, :, h * dv:(h + 1) * dv]
        y_ref[0, :, h * dv:(h + 1) * dv] = (_silu(gz) * on).astype(y_ref.dtype)


def ret_core(proj, gcs, S0, gn_w, c):
    B, L, _ = proj.shape
    H, dk, dv = S0.shape[1:]
    hb = 4
    nc = L // c
    gt = jnp.transpose(gcs, (0, 2, 1))
    gcol = gt[..., None]
    grow = gt.reshape(B, H, nc, 1, c)
    ng = H // hb
    assert (2 * H * dk) % (hb * dv) == 0
    v0 = 2 * H * dk // (hb * dv)
    q_spec = pl.BlockSpec((1, c, hb * dk), lambda b, g, n: (b, n, g))
    k_spec = pl.BlockSpec((1, c, hb * dk), lambda b, g, n: (b, n, ng + g))
    vin_spec = pl.BlockSpec((1, c, hb * dv), lambda b, g, n: (b, n, v0 + g))
    gz_spec = pl.BlockSpec((1, c, hb * dv), lambda b, g, n: (b, n, v0 + ng + g))
    v_spec = pl.BlockSpec((1, c, hb * dv), lambda b, g, n: (b, n, g))
    col_spec = pl.BlockSpec((1, hb, c, 1), lambda b, g, n: (b, g, n, 0))
    row_spec = pl.BlockSpec((1, hb, 1, 1, c), lambda b, g, n: (b, g, n, 0, 0))
    s_spec = pl.BlockSpec((1, hb, dk, dv), lambda b, g, n: (b, g, 0, 0))
    return pl.pallas_call(
        functools.partial(_ret_kernel, hb=hb, c=c, dk=dk, dv=dv),
        grid=(B, H // hb, nc),
        in_specs=[q_spec, k_spec, vin_spec, gz_spec, col_spec, row_spec, s_spec,
                  pl.BlockSpec((hb, 1, dv), lambda b, g, n: (g, 0, 0))],
        out_specs=[v_spec, s_spec],
        out_shape=[jax.ShapeDtypeStruct((B, L, H * dv), BF16), jax.ShapeDtypeStruct((B, H, dk, dv), F32)],
        compiler_params=_params(("parallel", "parallel", "arbitrary")),
    )(proj, proj, proj, proj, gcol, grow, S0, gn_w.reshape(H, 1, dv))


def _lru_kernel(x_ref, gi_ref, wg_ref, bg_ref, sp_ref, h0_ref, y_ref, hl_ref, a_s, b_s, hs_s, h_s, *, tl, seq):
    t = pl.program_id(2)
    x = x_ref[0]
    r = _sigmoid(_bdot(x, wg_ref[0, 0]) + bg_ref[0])
    ig = _sigmoid(_bdot(x, wg_ref[1, 0]) + bg_ref[1])
    log_a = -LRU_C * r * sp_ref[...]
    a = jnp.exp(log_a)
    b = jnp.sqrt(1.0 - jnp.exp(2.0 * log_a)) * (ig * x)
    if seq:
        @pl.when(t == 0)
        def _():
            h_s[...] = h0_ref[0]

        a_s[...] = a
        b_s[...] = b

        def body(i, h):
            h = a_s[pl.ds(i, 1), :] * h + b_s[pl.ds(i, 1), :]
            hs_s[pl.ds(i, 1), :] = h
            return h

        h = lax.fori_loop(0, tl, body, h_s[...])
        h_s[...] = h
        hl_ref[0] = h
        hs = hs_s[...]
    else:
        hs = a * h0_ref[0] + b
        hl_ref[0] = hs
    y_ref[0] = (hs * _gelu_tanh(gi_ref[0])).astype(y_ref.dtype)


def lru_core(xc, gate_in, wg2, bg, sp, h0, seq):
    B, L, C = xc.shape
    sb = wg2.shape[-1]
    ns = C // sb
    tl = min(L, 256)
    x_spec = pl.BlockSpec((1, tl, sb), lambda b, s, t: (b, t, s))
    h_rows = 1 if seq else tl
    h_spec = pl.BlockSpec((1, h_rows, sb), (lambda b, s, t: (b, 0, s)) if seq else (lambda b, s, t: (b, t, s)))
    return pl.pallas_call(
        functools.partial(_lru_kernel, tl=tl, seq=seq),
        grid=(B, ns, L // tl),
        in_specs=[x_spec, x_spec,
                  pl.BlockSpec((2, 1, sb, sb), lambda b, s, t: (0, s, 0, 0)),
                  pl.BlockSpec((2, 1, sb), lambda b, s, t: (0, 0, s)),
                  pl.BlockSpec((1, sb), lambda b, s, t: (0, s)),
                  h_spec],
        out_specs=[x_spec, h_spec],
        out_shape=[jax.ShapeDtypeStruct((B, L, C), BF16), jax.ShapeDtypeStruct((B, h_rows if seq else L, C), F32)],
        scratch_shapes=[pltpu.VMEM((tl, sb), F32), pltpu.VMEM((tl, sb), F32), pltpu.VMEM((tl, sb), F32),
                        pltpu.VMEM((1, sb), F32)],
        compiler_params=_params(("parallel", "parallel", "arbitrary")),
    )(xc, gate_in, wg2, bg, sp, h0)


def _moba_prompt_kernel(q_ref, k_ref, v_ref, sl_ref, o_ref, km_s, *, nb, blk, hd):
    qi = pl.program_id(2)

    @pl.when(qi == 0)
    def _():
        km_s[...] = jnp.zeros_like(km_s)
        for n in range(nb):
            km_s[n:n + 1, :] = jnp.mean(k_ref[0, n * blk:(n + 1) * blk, :], axis=0, keepdims=True)

    for qv in range(nb):
        @pl.when(qi == qv)
        def _(qv=qv):
            _moba_query_block(q_ref, k_ref, v_ref, sl_ref, o_ref, km_s, qv, blk, hd)


def _moba_query_block(q_ref, k_ref, v_ref, sl_ref, o_ref, km_s, qv, blk, hd):
    q = q_ref[0] * (hd ** -0.5)
    lane = lax.broadcasted_iota(jnp.int32, (blk, LANE), 1)
    sel = jnp.zeros((blk, LANE), F32)
    if qv > 0:
        gate = jnp.where(lane < qv, _bdot_nt(q, km_s[...]), -jnp.inf)
        for _ in range(min(MOBA_TOPK, qv)):
            gmax = jnp.max(gate, axis=-1, keepdims=True)
            imax = jnp.min(jnp.where(gate == gmax, lane, LANE), axis=-1, keepdims=True)
            sel = jnp.where(lane == imax, 1.0, sel)
            gate = jnp.where(lane == imax, -jnp.inf, gate)
    slope = sl_ref[0]
    rowi = lax.broadcasted_iota(jnp.int32, (blk, blk), 0)
    coli = lax.broadcasted_iota(jnp.int32, (blk, blk), 1)
    s_blocks = []
    for n in range(qv + 1):
        s = _bdot_nt(q, k_ref[0, n * blk:(n + 1) * blk, :])
        dist = (qv - n) * blk + rowi - coli
        ok = (dist >= 0) if n == qv else (sel[:, n:n + 1] > 0.0)
        s_blocks.append(jnp.where(ok, s - slope * dist.astype(F32), NEG_BIG))
    m = s_blocks[0].max(axis=-1, keepdims=True)
    for s in s_blocks[1:]:
        m = jnp.maximum(m, s.max(axis=-1, keepdims=True))
    l = jnp.zeros((blk, 1), F32)
    acc = jnp.zeros((blk, hd), F32)
    for n, s in enumerate(s_blocks):
        p = jnp.where(s > 0.5 * NEG_BIG, jnp.exp(s - m), 0.0)
        l = l + p.sum(axis=-1, keepdims=True)
        acc = acc + _bdot(p, v_ref[0, n * blk:(n + 1) * blk, :])
    o_ref[0] = (acc / l).astype(o_ref.dtype)


def moba_prompt(proj, slopes):
    B, L, D3 = proj.shape
    D = D3 // 3
    H = slopes.shape[0]
    hd = D // H
    blk = MOBA_BLOCK
    nb = L // blk
    q_spec = pl.BlockSpec((1, blk, hd), lambda b, h, i: (b, i, h))
    return pl.pallas_call(
        functools.partial(_moba_prompt_kernel, nb=nb, blk=blk, hd=hd),
        grid=(B, H, nb),
        in_specs=[q_spec,
                  pl.BlockSpec((1, L, hd), lambda b, h, i: (b, 0, H + h)),
                  pl.BlockSpec((1, L, hd), lambda b, h, i: (b, 0, 2 * H + h)),
                  pl.BlockSpec((1, 1, 1), lambda b, h, i: (h, 0, 0))],
        out_specs=q_spec,
        out_shape=jax.ShapeDtypeStruct((B, L, D), BF16),
        scratch_shapes=[pltpu.VMEM((LANE, hd), F32)],
        compiler_params=_params(("parallel", "parallel", "arbitrary")),
    )(proj, proj, proj, slopes.reshape(H, 1, 1))


def _kblock_sum_kernel(pt_ref, k_ref, o_ref, *, ppb):
    p = pl.program_id(1)

    @pl.when(p % ppb == 0)
    def _():
        o_ref[...] = jnp.zeros_like(o_ref)

    o_ref[0, 0] += jnp.sum(k_ref[0, 0], axis=0)


def kblock_sums(cache_k, page_table, layer):
    n_seq, n_pages = page_table.shape
    _, _, page, H, hd = cache_k.shape
    ppb = MOBA_BLOCK // page
    grid_spec = pltpu.PrefetchScalarGridSpec(
        num_scalar_prefetch=1, grid=(n_seq, n_pages),
        in_specs=[pl.BlockSpec((1, 1, page, H, hd), lambda b, p, pt: (layer, pt[b, p], 0, 0, 0))],
        out_specs=pl.BlockSpec((1, 1, H, hd), lambda b, p, pt: (b, p // ppb, 0, 0)))
    return pl.pallas_call(
        functools.partial(_kblock_sum_kernel, ppb=ppb), grid_spec=grid_spec,
        out_shape=jax.ShapeDtypeStruct((n_seq, n_pages // ppb, H, hd), F32),
        compiler_params=_params(("parallel", "arbitrary")),
    )(page_table, cache_k)


def _moba_step_kernel(pt_ref, q_ref, kn_ref, vn_ref, ks_ref, sl_ref, k_ref, v_ref, o_ref, m_s, l_s, acc_s, sel_s,
                      *, page, n_pages, hd):
    p = pl.program_id(1)
    q = q_ref[0] * (hd ** -0.5)

    @pl.when(p == 0)
    def _():
        m_s[...] = jnp.sum(q * kn_ref[0], axis=-1, keepdims=True)
        l_s[...] = jnp.ones_like(l_s)
        acc_s[...] = vn_ref[0]
        gate = jnp.sum(ks_ref[0] * (1.0 / MOBA_BLOCK) * q[None], axis=-1, keepdims=True)
        blk = lax.broadcasted_iota(jnp.int32, gate.shape, 0).astype(F32)
        sel = jnp.zeros_like(gate)
        for _ in range(min(MOBA_TOPK, gate.shape[0])):
            gmax = jnp.max(gate, axis=0, keepdims=True)
            imax = jnp.min(jnp.where(gate == gmax, blk, float(2 * gate.shape[0])), axis=0, keepdims=True)
            sel = jnp.where(blk == imax, 1.0, sel)
            gate = jnp.where(blk == imax, -jnp.inf, gate)
        sel_s[...] = sel

    k = k_ref[0, 0]
    s = jnp.sum(k * q[None], axis=-1, keepdims=True)
    tok = lax.broadcasted_iota(jnp.int32, s.shape, 0)
    dist = (n_pages * page - (p * page + tok)).astype(F32)
    ok = sel_s[p // (MOBA_BLOCK // page)][None] > 0.0
    s = jnp.where(ok, s - sl_ref[...][None] * dist, NEG_BIG)
    m_old = m_s[...]
    m_new = jnp.maximum(m_old, jnp.max(s, axis=0))
    alpha = jnp.exp(m_old - m_new)
    pr = jnp.where(ok, jnp.exp(s - m_new[None]), 0.0)
    l_s[...] = alpha * l_s[...] + jnp.sum(pr, axis=0)
    acc_s[...] = alpha * acc_s[...] + jnp.sum(pr * v_ref[0, 0], axis=0)
    m_s[...] = m_new

    @pl.when(p == n_pages - 1)
    def _():
        o_ref[0] = (acc_s[...] / l_s[...]).astype(o_ref.dtype)


def moba_step(q, k_new, v_new, k_sums, slopes, cache_k, cache_v, page_table, layer):
    n_seq, n_pages = page_table.shape
    _, _, page, H, hd = cache_k.shape
    n_blk = k_sums.shape[1]
    assert n_blk * MOBA_BLOCK == n_pages * page
    vec = pl.BlockSpec((1, H, hd), lambda b, p, pt: (b, 0, 0))
    kv = pl.BlockSpec((1, 1, page, H, hd), lambda b, p, pt: (layer, pt[b, p], 0, 0, 0))
    grid_spec = pltpu.PrefetchScalarGridSpec(
        num_scalar_prefetch=1, grid=(n_seq, n_pages),
        in_specs=[vec, vec, vec,
                  pl.BlockSpec((1, n_blk, H, hd), lambda b, p, pt: (b, 0, 0, 0)),
                  pl.BlockSpec((H, 1), lambda b, p, pt: (0, 0)),
                  kv, kv],
        out_specs=vec,
        scratch_shapes=[pltpu.VMEM((H, 1), F32), pltpu.VMEM((H, 1), F32), pltpu.VMEM((H, hd), F32),
                        pltpu.VMEM((n_blk, H, 1), F32)])
    return pl.pallas_call(
        functools.partial(_moba_step_kernel, page=page, n_pages=n_pages, hd=hd), grid_spec=grid_spec,
        out_shape=jax.ShapeDtypeStruct((n_seq, H, hd), BF16),
        compiler_params=_params(("parallel", "arbitrary")),
    )(page_table, q, k_new, v_new, k_sums, slopes.reshape(H, 1), cache_k, cache_v)


def _top_rows(s, pos, k):
    vals, idxs = [], []
    for _ in range(k):
        mx = jnp.max(s, axis=0, keepdims=True)
        ix = jnp.min(jnp.where(s == mx, pos, 1e9), axis=0, keepdims=True)
        s = jnp.where(pos == ix, -jnp.inf, s)
        vals.append(mx)
        idxs.append(ix)
    return vals, idxs


def _peer_route_kernel(q_ref, keys_ref, g_ref, r_ref, c_ref, *, dsub, tt):
    key_pos = lax.broadcasted_iota(jnp.int32, (N_KEYS, tt), 0).astype(F32)
    s0 = _bdot_nt(keys_ref[0], q_ref[:, :dsub])
    s1 = _bdot_nt(keys_ref[1], q_ref[:, dsub:])
    v0, i0 = _top_rows(s0, key_pos, P_TOPK)
    v1, i1 = _top_rows(s1, key_pos, P_TOPK)
    top_s1 = jnp.concatenate(v1, axis=0)
    top_i0 = jnp.concatenate(i0, axis=0)
    top_i1 = jnp.concatenate(i1, axis=0)
    assert P_TOPK == 2 * SUBLANE
    sub = lax.broadcasted_iota(jnp.int32, (SUBLANE, tt), 0).astype(F32)
    blocks = [v0[0] + top_s1]
    poss = [lax.broadcasted_iota(jnp.int32, (P_TOPK, tt), 0).astype(F32)]
    for a in range(1, SUBLANE):
        blk = v0[a] + top_s1[:SUBLANE]
        blocks.append(jnp.where(sub < float(P_TOPK // (a + 1)), blk, -jnp.inf))
        poss.append(sub + float(a * P_TOPK))
    blocks.append(jnp.concatenate(v0[SUBLANE:], axis=0) + v1[0])
    poss.append((sub + float(SUBLANE)) * float(P_TOPK))
    best_s, best_pos = _top_rows(jnp.concatenate(blocks, axis=0), jnp.concatenate(poss, axis=0), P_TOPK)
    slot = lax.broadcasted_iota(jnp.int32, (P_TOPK, tt), 0).astype(F32)
    e = [jnp.exp(b - best_s[0]) for b in best_s]
    z = e[0]
    for x in e[1:]:
        z = z + x
    for r in range(P_TOPK):
        a = jnp.floor(best_pos[r] * (1.0 / P_TOPK))
        b = best_pos[r] - a * P_TOPK
        g_ref[r:r + 1, :] = e[r] / z
        r_ref[r:r + 1, :] = jnp.sum(jnp.where(slot == a, top_i0, 0.0), axis=0, keepdims=True)
        c_ref[r:r + 1, :] = jnp.sum(jnp.where(slot == b, top_i1, 0.0), axis=0, keepdims=True)


def peer_route(q, keys):
    T = q.shape[0]
    dsub = keys.shape[-1]
    n_heads = keys.shape[0] // 2
    tt = LANE
    out_spec = pl.BlockSpec((P_TOPK, tt), lambda i, h: (h, i))
    shape = jax.ShapeDtypeStruct((n_heads * P_TOPK, T), F32)
    return pl.pallas_call(
        functools.partial(_peer_route_kernel, dsub=dsub, tt=tt),
        grid=(T // tt, n_heads),
        in_specs=[pl.BlockSpec((tt, 2 * dsub), lambda i, h: (i, h)),
                  pl.BlockSpec((2, N_KEYS, dsub), lambda i, h: (h, 0, 0))],
        out_specs=[out_spec, out_spec, out_spec], out_shape=[shape, shape, shape],
        compiler_params=_params(("parallel", "parallel")),
    )(q, keys)


def _peer_gates_kernel(g_ref, i_ref, j_ref, o_ref, m_s, *, tb):
    iota = lax.broadcasted_iota(jnp.int32, (tb, N_KEYS, LANE), 1).astype(F32)
    a = jnp.where(i_ref[...] == iota, g_ref[...], 0.0)
    a_hi = a.astype(BF16)
    a_lo = (a - a_hi.astype(F32)).astype(BF16)
    bt = jnp.where(j_ref[...] == iota, 1.0, 0.0).astype(BF16)
    dn = (((2,), (2,)), ((0,), (0,)))
    both = lax.dot_general(jnp.concatenate([a_hi, a_lo], axis=1), bt, dn, preferred_element_type=F32)
    m_s[...] = both[:, :N_KEYS] + both[:, N_KEYS:]
    low = [(lax.broadcasted_iota(jnp.int32, (SUBLANE, LANE), 0) & s) == 0 for s in (4, 2, 1)]
    for tg in range(tb // SUBLANE):
        for a in range(N_KEYS // SUBLANE):
            v = [m_s[tg * SUBLANE + r, a * SUBLANE:(a + 1) * SUBLANE, :] for r in range(SUBLANE)]
            for keep, s in zip(low, (4, 2, 1)):
                for k in range(SUBLANE):
                    if k & s == 0:
                        x, y = v[k], v[k + s]
                        v[k] = jnp.where(keep, x, pltpu.roll(y, s, 0))
                        v[k + s] = jnp.where(keep, pltpu.roll(x, SUBLANE - s, 0), y)
            for c in range(SUBLANE):
                i = a * SUBLANE + c
                o_ref[tg * SUBLANE:(tg + 1) * SUBLANE, i * N_KEYS:(i + 1) * N_KEYS] = v[c]


def peer_gate_matrix(gates, rows, cols):
    T, slots = gates.shape
    assert slots == LANE
    tb = min(T, 32)
    spec = pl.BlockSpec((tb, 1, slots), lambda i: (i, 0, 0))
    return pl.pallas_call(
        functools.partial(_peer_gates_kernel, tb=tb),
        grid=(T // tb,), in_specs=[spec, spec, spec],
        out_specs=pl.BlockSpec((tb, N_KEYS * N_KEYS), lambda i: (i, 0)),
        out_shape=jax.ShapeDtypeStruct((T, N_KEYS * N_KEYS), F32),
        scratch_shapes=[pltpu.VMEM((tb, N_KEYS, N_KEYS), F32)],
        compiler_params=_params(("parallel",)),
    )(gates.reshape(T, 1, slots), rows.reshape(T, 1, slots), cols.reshape(T, 1, slots))


def _peer_experts_kernel(x_ref, g_ref, u_ref, v_ref, o_ref):
    e = pl.program_id(1)

    @pl.when(e == 0)
    def _():
        o_ref[...] = jnp.zeros_like(o_ref)

    act = _gelu_tanh(_bdot_nt(x_ref[...], u_ref[...]))
    o_ref[...] += _bdot(g_ref[...] * act, v_ref[...])


def peer_experts(x, gmat, u, v):
    T, D = x.shape
    E = u.shape[0]
    tm = min(T, 512)
    te = 512
    return pl.pallas_call(
        _peer_experts_kernel,
        grid=(T // tm, E // te),
        in_specs=[pl.BlockSpec((tm, D), lambda i, e: (i, 0)), pl.BlockSpec((tm, te), lambda i, e: (i, e)),
                  pl.BlockSpec((te, D), lambda i, e: (e, 0)), pl.BlockSpec((te, D), lambda i, e: (e, 0))],
        out_specs=pl.BlockSpec((tm, D), lambda i, e: (i, 0)),
        out_shape=jax.ShapeDtypeStruct((T, D), F32),
        compiler_params=_params(("parallel", "arbitrary")),
    )(x, gmat, u, v)


def peer_ffn(hs, wq, keys, u, v):
    n_heads = keys.shape[0]
    dsub = keys.shape[-1]
    outs = []
    for h, q in zip(hs, proj_pair(hs, wq)):
        B, L, D = h.shape
        T = B * L
        q = jnp.pad(q.reshape(T, -1), ((0, -T % LANE), (0, 0)))
        gates, rows, cols = (t[:, :T].T for t in peer_route(q, keys.reshape(n_heads * 2, N_KEYS, dsub)))
        outs.append(peer_experts(h.reshape(T, D), peer_gate_matrix(gates, rows, cols), u, v).reshape(B, L, D))
    return outs


def _pad_rows(t, rows):
    return jnp.pad(t, ((0, 0), (0, rows - t.shape[1])) + ((0, 0),) * (t.ndim - 2))


def _chunk_cumsum(g, c):
    B, L, H = g.shape
    return jnp.cumsum(g.reshape(B, L // c, c, H), axis=2).reshape(B, L, H)


def gdn_mixer(hs, states, params):
    w_in, w_out = params[0], params[-1]
    H, dk, dv = states[0][0].shape[1:]
    n_main = 2 * H * dk + 2 * H * dv
    projs = proj_pair(hs, w_in, n_cols=n_main)
    bas = proj_pair(hs, w_in[:, n_main:])
    ys, new_states = zip(*(_gdn_group(projs[g], bas[g], states[g], params) for g in range(2)))
    return proj_pair(ys, w_out), new_states


def _gdn_group(proj, ba, state, params):
    S0, conv0 = state
    _, conv_w, a_log, dt_bias, norm_w, _ = params
    B, L, _ = proj.shape
    H, dk, dv = S0.shape[1:]
    conv_dim = 2 * H * dk + H * dv
    conv_new = jnp.concatenate([conv0, proj[:, max(L - (CONV_W - 1), 0):, :conv_dim]], axis=1)[:, -(CONV_W - 1):]
    beta = jax.nn.sigmoid(ba[..., :H])
    g = -jnp.exp(a_log) * jax.nn.softplus(ba[..., H:] + dt_bias)
    c = math.gcd(L, GDN_CHUNK)
    if c < SUBLANE:
        c = SUBLANE
        proj, g, beta = (_pad_rows(t, c) for t in (proj, g, beta))
    qkv = dwconv(proj, 0, conv0, conv_w, gdn_heads=(H, dk))
    y, S = gdn_core(qkv, proj, _chunk_cumsum(g, c), beta, S0, norm_w, c)
    return y[:, :L], (S, conv_new)


def lru_mixer(hs, states, params):
    w_in, w_out = params[0], params[-1]
    projs = proj_pair(hs, w_in)
    ys, new_states = zip(*(_lru_group(projs[g], states[g], params) for g in range(2)))
    return proj_pair(ys, w_out), new_states


def _lru_group(proj, state, params):
    h0, conv0 = state
    _, conv_w, conv_b, w_gates, b_gates, lam, _ = params
    B, L, _ = proj.shape
    C = h0.shape[-1]
    conv_new = jnp.concatenate([conv0, proj[:, max(L - (CONV_W - 1), 0):, C:]], axis=1)[:, -(CONV_W - 1):]
    xc = dwconv(_pad_rows(proj, max(L, SUBLANE)), C, conv0, conv_w, bias=conv_b)[:, :L]
    nb, bw = w_gates.shape[1], w_gates.shape[2]
    per = (bw * LANE // math.gcd(bw, LANE)) // bw
    wg = w_gates.reshape(2, nb // per, per, bw, bw)
    eye = jnp.eye(per, dtype=w_gates.dtype)
    wg2 = jnp.einsum('gspij,pq->gspiqj', wg, eye).reshape(2, nb // per, per * bw, per * bw)
    sp = jax.nn.softplus(-lam).reshape(1, C)
    bg = b_gates.reshape(2, 1, C)
    if L > 1:
        y, hl = lru_core(xc, proj, wg2, bg, sp, h0.reshape(B, 1, C), True)
        h_last = hl.reshape(B, C)
    else:
        y, hl = lru_core(xc.reshape(1, B, C), proj.reshape(1, B, 2 * C), wg2, bg, sp, h0.reshape(1, B, C), False)
        y = y.reshape(B, 1, C)
        h_last = hl.reshape(B, C)
    return y, (h_last, conv_new)


def ret_mixer(hs, states, params):
    w_in, gn_w, w_out = params
    projs = proj_pair(hs, w_in)
    ys, new_states = zip(*(_ret_group(projs[g], states[g], gn_w) for g in range(2)))
    return proj_pair(ys, w_out), new_states


def _ret_group(proj, state, gn_w):
    (S0,) = state
    B, L, _ = proj.shape
    H = S0.shape[1]
    log_gamma = jnp.log1p(-jnp.exp2(-5.0 - jnp.arange(H, dtype=F32)))
    c = math.gcd(L, RET_CHUNK)
    Lp = L
    if c < SUBLANE:
        Lp = c = SUBLANE
        proj = _pad_rows(proj, Lp)
    pos = jnp.arange(Lp)
    steps = jnp.minimum(pos % c + 1, jnp.maximum(L - (pos // c) * c, 0)).astype(F32)
    gcs = jnp.broadcast_to(steps[None, :, None] * log_gamma[None, None, :], (B, Lp, H))
    y, S = ret_core(proj, gcs, S0, gn_w, c)
    return y[:, :L], (S,)


def moba_mixer(hs, pasts, params, n_heads):
    w_in, w_out = params
    projs = proj_pair(hs, w_in)
    ys, new_states = zip(*(_moba_group(projs[g], pasts[g], n_heads) for g in range(2)))
    return proj_pair(ys, w_out), new_states


def _moba_group(proj, past, n_heads):
    B, L, D3 = proj.shape
    D = D3 // 3
    hd = D // n_heads
    q, k, v = proj[..., :D], proj[..., D:2 * D], proj[..., 2 * D:]
    slopes = jnp.exp2(-8.0 * (jnp.arange(n_heads, dtype=F32) + 1.0) / n_heads)
    if past is None:
        o = moba_prompt(proj, slopes)
    else:
        cache_k, cache_v, page_table, layer = past
        k_sums = kblock_sums(cache_k, page_table, layer)
        o = moba_step(q.reshape(B, n_heads, hd), k.reshape(B, n_heads, hd), v.reshape(B, n_heads, hd), k_sums,
                      slopes, cache_k, cache_v, page_table, layer).reshape(B, L, D)
    return o, (k.reshape(B, L, n_heads, hd), v.reshape(B, L, n_heads, hd))


def kernel(x_prompt, x_sample, c_prompt, c_sample, state_gdn_S, state_gdn_conv, state_lru_h, state_lru_conv, state_ret_S, cache_k, cache_v, page_table, w_ada, b_ada, ln_g, ln_b, peer_wq, peer_keys, peer_u, peer_v, gdn_w_in, gdn_conv_w, gdn_a_log, gdn_dt_bias, gdn_norm_w, gdn_w_out, lru_w_in, lru_conv_w, lru_conv_b, lru_w_gates, lru_b_gates, lru_lambda, lru_w_out, ret_w_in, ret_gn_w, ret_w_out, moba_w_in, moba_w_out):
    depth, D = w_ada.shape[0], w_ada.shape[1]
    n_mixers = 4
    Bp, Bs = x_prompt.shape[0], x_sample.shape[0]
    H_A, DK_A, DV_A = state_gdn_S.shape[2:]
    conv_dim_a = state_gdn_conv.shape[-1]
    C_B = state_lru_h.shape[-1]
    H_C, DK_C, DV_C = state_ret_S.shape[2:]
    H_D = cache_k.shape[3]

    c_all = jax.nn.silu(jnp.concatenate([c_prompt, c_sample], axis=0))
    n_c = c_all.shape[0]
    c_pad = jnp.pad(c_all, ((0, -n_c % SUBLANE), (0, 0)))
    xs = [x_prompt, x_sample]
    groups = ((0, Bp), (Bp, Bs))
    new_states = {(m, grp): [] for m in range(n_mixers) for grp in range(2)}
    hs = [None, None]
    mod_all = [matmul(c_pad, w_ada[i], bias=b_ada[i]) for i in range(depth)]
    mods = [[[mod[s:s + n, None, r * D:(r + 1) * D] for r in range(6)] for s, n in groups] for mod in mod_all]
    for i in range(depth):
        m, j = i % n_mixers, i // n_mixers
        peer_params = (peer_wq[i], peer_keys[i], peer_u[i].astype(BF16), peer_v[i].astype(BF16))
        if i == 0:
            hs = [modulate(xs[g], mods[0][g][1], mods[0][g][0]) for g in range(2)]
        if m == 0:
            params = (gdn_w_in[j], gdn_conv_w[j], gdn_a_log[j], gdn_dt_bias[j], gdn_norm_w[j], gdn_w_out[j])
            states = ((jnp.zeros((Bp, H_A, DK_A, DV_A), F32), jnp.zeros((Bp, CONV_W - 1, conv_dim_a), F32)),
                      (state_gdn_S[j], state_gdn_conv[j]))
            ys, ns = gdn_mixer(hs, states, params)
        elif m == 1:
            params = (lru_w_in[j], lru_conv_w[j], lru_conv_b[j], lru_w_gates[j], lru_b_gates[j], lru_lambda[j],
                      lru_w_out[j])
            states = ((jnp.zeros((Bp, C_B), F32), jnp.zeros((Bp, CONV_W - 1, C_B), F32)),
                      (state_lru_h[j], state_lru_conv[j]))
            ys, ns = lru_mixer(hs, states, params)
        elif m == 2:
            states = ((jnp.zeros((Bp, H_C, DK_C, DV_C), F32),), (state_ret_S[j],))
            ys, ns = ret_mixer(hs, states, (ret_w_in[j], ret_gn_w[j], ret_w_out[j]))
        else:
            ys, ns = moba_mixer(hs, (None, (cache_k, cache_v, page_table, j)), (moba_w_in[j], moba_w_out[j]), H_D)
        x1, h2 = [], []
        for g in range(2):
            new_states[(m, g)].append(ns[g])
            _, _, g1, sh2, sc2, _ = mods[i][g]
            xg, hg = ln_residual(xs[g], ys[g], g1, sc2, sh2, ln_g[i, 0], ln_b[i, 0])
            x1.append(xg)
            h2.append(hg)
        y2 = peer_ffn(h2, *peer_params)
        for g in range(2):
            if i + 1 < depth:
                sh_n, sc_n = mods[i + 1][g][0], mods[i + 1][g][1]
            else:
                sh_n = sc_n = jnp.zeros((xs[g].shape[0], 1, D), F32)
            xs[g], hs[g] = ln_residual(x1[g], y2[g], mods[i][g][5], sc_n, sh_n, ln_g[i, 1], ln_b[i, 1])

    def stacked(m, grp):
        return [jnp.stack(parts) for parts in zip(*new_states[(m, grp)])]

    gdn_S_p, gdn_conv_p = stacked(0, 0)
    gdn_S_s, gdn_conv_s = stacked(0, 1)
    lru_h_p, lru_conv_p = stacked(1, 0)
    lru_h_s, lru_conv_s = stacked(1, 1)
    (ret_S_p,) = stacked(2, 0)
    (ret_S_s,) = stacked(2, 1)
    k_p, v_p = stacked(3, 0)
    k_s, v_s = stacked(3, 1)
    return (xs[0], xs[1], gdn_S_p, gdn_S_s, gdn_conv_p, gdn_conv_s, lru_h_p, lru_h_s, lru_conv_p, lru_conv_s,
            ret_S_p, ret_S_s, k_p, k_s, v_p, v_s)
```

```python
import functools
import math

import jax
import jax.numpy as jnp
from jax import lax
from jax.experimental import pallas as pl
from jax.experimental.pallas import tpu as pltpu

F32 = jnp.float32
BF16 = jnp.bfloat16

DEPTH = 4
ALPHA = (2.0 * DEPTH) ** 0.25
CONV_W = 4
LN_EPS = 1e-5
NORM_EPS = 1e-6
GDN_CHUNK = 64
RET_CHUNK = 64
LRU_C = 8.0
MOBA_BLOCK = 256
MOBA_TOPK = 3
P_TOPK = 16
N_KEYS = 128
LANE = 128
SUBLANE = 8
VMEM_LIMIT = 56 * 1024 * 1024
NEG_BIG = -1e30

_NT = (((1,), (1,)), ((), ()))
_TN = (((0,), (0,)), ((), ()))


def _tile(n, cap, align):
    for t in range(min(cap, n) // align * align, 0, -align):
        if n % t == 0:
            return t
    return n


def _params(sem):
    return pltpu.CompilerParams(dimension_semantics=sem, vmem_limit_bytes=VMEM_LIMIT)


def _bdot(a, b):
    return jnp.dot(a.astype(BF16), b.astype(BF16), preferred_element_type=F32)


def _bdot_nt(a, b):
    return lax.dot_general(a.astype(BF16), b.astype(BF16), _NT, preferred_element_type=F32)


def _bdot_tn(a, b):
    return lax.dot_general(a.astype(BF16), b.astype(BF16), _TN, preferred_element_type=F32)


def _fdot(a, b):
    return jnp.dot(a, b, preferred_element_type=F32, precision=lax.Precision.HIGHEST)


def _gelu_tanh(x):
    return 0.5 * x * (1.0 + jnp.tanh(math.sqrt(2.0 / math.pi) * (x + 0.044715 * (x * x * x))))


def _sigmoid(x):
    return 1.0 / (1.0 + jnp.exp(-x))


def _silu(x):
    return x * _sigmoid(x)


def _mm_kernel(a_ref, b_ref, *rest, nk, has_bias):
    if has_bias:
        bias_ref, o_ref, acc_ref = rest
    else:
        o_ref, acc_ref = rest
    k = pl.program_id(2)

    @pl.when(k == 0)
    def _():
        acc_ref[...] = jnp.zeros_like(acc_ref)

    acc_ref[...] += _bdot(a_ref[...], b_ref[...])

    @pl.when(k == nk - 1)
    def _():
        r = acc_ref[...]
        if has_bias:
            r = r + bias_ref[...]
        o_ref[...] = r.astype(o_ref.dtype)


def matmul(a, b, layer, *, bias=None, out_dtype=F32):
    M, K = a.shape
    N = b.shape[2]
    tm = _tile(M, 1024, SUBLANE)
    tn = _tile(N, 1024 if M > 64 else 2048, LANE)
    tk = _tile(K, 512, LANE)
    nk = K // tk
    in_specs = [pl.BlockSpec((tm, tk), lambda i, j, k: (i, k)),
                pl.BlockSpec((None, tk, tn), lambda i, j, k: (layer, k, j))]
    args = [a, b]
    if bias is not None:
        in_specs.append(pl.BlockSpec((1, tn), lambda i, j, k: (0, j)))
        args.append(bias.reshape(1, -1))
    return pl.pallas_call(
        functools.partial(_mm_kernel, nk=nk, has_bias=bias is not None),
        grid=(M // tm, N // tn, nk),
        in_specs=in_specs,
        out_specs=pl.BlockSpec((tm, tn), lambda i, j, k: (i, j)),
        out_shape=jax.ShapeDtypeStruct((M, N), out_dtype),
        scratch_shapes=[pltpu.VMEM((tm, tn), F32)],
        compiler_params=_params(("parallel", "parallel", "arbitrary")),
    )(*args)


def _mm_pair_kernel(ap_ref, as_ref, b_ref, op_ref, os_ref, b_s):
    @pl.when(pl.program_id(1) == 0)
    def _():
        b_s[...] = b_ref[...].astype(BF16)
        os_ref[...] = jnp.dot(as_ref[...].astype(BF16), b_s[...], preferred_element_type=F32)

    op_ref[...] = jnp.dot(ap_ref[...].astype(BF16), b_s[...], preferred_element_type=F32)


def matmul_pair(a_p, a_s, b, n_cols=None, layer=None):
    Mp, K = a_p.shape
    Ms = a_s.shape[0]
    N = b.shape[-1] if n_cols is None else n_cols
    wide = K <= 5120
    tm = _tile(Mp, 1024 if wide else 512, SUBLANE)
    tn = _tile(N, 512 if wide else 256, LANE)
    if layer is None:
        b_spec = pl.BlockSpec((K, tn), lambda j, i: (0, j))
    else:
        b_spec = pl.BlockSpec((None, K, tn), lambda j, i: (layer, 0, j))
    return pl.pallas_call(
        _mm_pair_kernel,
        grid=(N // tn, Mp // tm),
        in_specs=[pl.BlockSpec((tm, K), lambda j, i: (i, 0)), pl.BlockSpec((Ms, K), lambda j, i: (0, 0)), b_spec],
        out_specs=[pl.BlockSpec((tm, tn), lambda j, i: (i, j)), pl.BlockSpec((Ms, tn), lambda j, i: (0, j))],
        out_shape=[jax.ShapeDtypeStruct((Mp, N), F32), jax.ShapeDtypeStruct((Ms, N), F32)],
        scratch_shapes=[pltpu.VMEM((K, tn), BF16)],
        compiler_params=_params(("parallel", "arbitrary")),
    )(a_p, a_s, b)


def proj_pair(xs, w, n_cols=None, layer=None):
    (Bp, Lp, K), (Bs, Ls, _) = xs[0].shape, xs[1].shape
    o_p, o_s = matmul_pair(xs[0].reshape(Bp * Lp, K), xs[1].reshape(Bs * Ls, K), w, n_cols, layer)
    return o_p.reshape(Bp, Lp, -1), o_s.reshape(Bs, Ls, -1)


def _modulate_kernel(x_ref, sc_ref, sh_ref, h_ref):
    h_ref[0] = (x_ref[0] * (1.0 + sc_ref[0]) + sh_ref[0]).astype(h_ref.dtype)


def modulate(x, sc, sh):
    B, L, D = x.shape
    tl = min(L, 256)
    row = pl.BlockSpec((1, tl, D), lambda b, t: (b, t, 0))
    vec = pl.BlockSpec((1, 1, D), lambda b, t: (b, 0, 0))
    return pl.pallas_call(
        _modulate_kernel, grid=(B, L // tl), in_specs=[row, vec, vec], out_specs=row,
        out_shape=jax.ShapeDtypeStruct((B, L, D), BF16),
        compiler_params=_params(("parallel", "parallel")),
    )(x, sc, sh)


def _ln_kernel(x_ref, y_ref, g_ref, sc_ref, sh_ref, lg_ref, lb_ref, xo_ref, ho_ref):
    z = ALPHA * x_ref[0] + (1.0 + g_ref[0]) * y_ref[0]
    mu = jnp.mean(z, axis=-1, keepdims=True)
    zc = z - mu
    var = jnp.mean(zc * zc, axis=-1, keepdims=True)
    xn = zc * lax.rsqrt(var + LN_EPS) * lg_ref[...] + lb_ref[...]
    xo_ref[0] = xn
    ho_ref[0] = (xn * (1.0 + sc_ref[0]) + sh_ref[0]).astype(ho_ref.dtype)


def ln_residual(x, y, gate, sc_next, sh_next, ln_g, ln_b):
    B, L, D = x.shape
    tl = min(L, 256)
    row = pl.BlockSpec((1, tl, D), lambda b, t: (b, t, 0))
    vec = pl.BlockSpec((1, 1, D), lambda b, t: (b, 0, 0))
    par = pl.BlockSpec((1, D), lambda b, t: (0, 0))
    return pl.pallas_call(
        _ln_kernel, grid=(B, L // tl),
        in_specs=[row, row, vec, vec, vec, par, par], out_specs=[row, row],
        out_shape=[jax.ShapeDtypeStruct((B, L, D), F32), jax.ShapeDtypeStruct((B, L, D), BF16)],
        compiler_params=_params(("parallel", "parallel")),
    )(x, y, gate, sc_next, sh_next, ln_g.reshape(1, D), ln_b.reshape(1, D))


def _split_bf16(a):
    hi = a.astype(BF16)
    return hi, (a - hi.astype(F32)).astype(BF16)


def _dot3(a, b):
    ah, al = _split_bf16(a)
    bh, bl = _split_bf16(b)
    dot = functools.partial(jnp.dot, preferred_element_type=F32)
    return dot(ah, bh) + (dot(ah, bl) + dot(al, bh))


def _head_blocks(x, row_head, n):
    return jnp.concatenate([jnp.where(row_head == j, x, 0.0) for j in range(n)], axis=1)


def _gdn_kernel(q_ref, k_ref, v_ref, z_ref, gb_ref, s0_ref, nw_ref, y_ref, s_ref, *, nst, hs, c, dk, dv):
    n = pl.program_id(2)

    @pl.when(n == 0)
    def _():
        s_ref[...] = s0_ref[...]

    R = hs * c
    shift = int(math.log2(c))
    row = lax.broadcasted_iota(jnp.int32, (R, R), 0)
    col = lax.broadcasted_iota(jnp.int32, (R, R), 1)
    same = jnp.right_shift(row, shift) == jnp.right_shift(col, shift)
    tri = same & (row >= col)
    strict = same & (row > col)
    diag = row == col
    eye = diag.astype(F32)
    row_head = jnp.right_shift(lax.broadcasted_iota(jnp.int32, (R, 1), 0), shift)

    def as_column(x_row):
        return jnp.sum(jnp.where(diag, x_row, 0.0), axis=-1, keepdims=True)

    for st in range(nst):
        heads = range(st * hs, (st + 1) * hs)
        q = jnp.concatenate([q_ref[0, :, h * dk:(h + 1) * dk] for h in heads], axis=0)
        k = jnp.concatenate([k_ref[0, :, h * dk:(h + 1) * dk] for h in heads], axis=0)
        v = jnp.concatenate([v_ref[0, :, h * dv:(h + 1) * dv] for h in heads], axis=0)
        z = jnp.concatenate([z_ref[0, :, h * dv:(h + 1) * dv] for h in heads], axis=0)
        grw = gb_ref[0, st, 0, 0:1, :]
        gcl = as_column(grw)
        bt = as_column(gb_ref[0, st, 0, 1:2, :])
        g_ends = [grw[:, (j + 1) * c - 1:(j + 1) * c] for j in range(hs)]
        g_end = g_ends[0]
        for j in range(1, hs):
            g_end = jnp.where(row_head == j, g_ends[j], g_end)
        decay = jnp.where(tri, jnp.exp(jnp.where(tri, gcl - grw, 0.0)), 0.0)
        kb = k * bt
        m = jnp.where(strict, _bdot_nt(kb, k) * decay, 0.0)
        tinv = eye - m
        p = m
        for _ in range(shift - 1):
            p = _dot3(p, p)
            tinv = tinv + _dot3(tinv, p)
        u = _dot3(tinv, v * bt)
        w = _dot3(tinv, kb * jnp.exp(gcl))
        qk = jnp.where(tri, _bdot_nt(q, k) * decay, 0.0)
        S = jnp.concatenate([s_ref[0, h] for h in heads], axis=0)
        v_new = u - _bdot(_head_blocks(w, row_head, hs), S)
        o = _bdot(_head_blocks(q * jnp.exp(gcl), row_head, hs), S) + _bdot(qk, v_new)
        s_add = _bdot_tn(_head_blocks(k * jnp.exp(g_end - gcl), row_head, hs), v_new)
        o = o * lax.rsqrt(jnp.mean(o * o, axis=-1, keepdims=True) + NORM_EPS) * nw_ref[...]
        y = (o * _silu(z)).astype(y_ref.dtype)
        for j, h in enumerate(heads):
            s_ref[0, h] = S[j * dk:(j + 1) * dk] * jnp.exp(g_ends[j]) + s_add[j * dk:(j + 1) * dk]
            y_ref[0, :, h * dv:(h + 1) * dv] = y[j * c:(j + 1) * c]


def _dwconv_kernel(x_ref, prev_ref, c0_ref, w_ref, b_ref, o_ref, xs, *, tl, gdn, dk, nq):
    t, j = pl.program_id(1), pl.program_id(2)
    xs[0:SUBLANE, :] = jnp.where(t == 0, c0_ref[0], prev_ref[0])
    xs[SUBLANE:SUBLANE + tl, :] = x_ref[0]
    first = SUBLANE - (CONV_W - 1)
    y = xs[first:first + tl, :] * w_ref[0:1, :]
    for r in range(1, CONV_W):
        y = y + xs[first + r:first + r + tl, :] * w_ref[r:r + 1, :]
    if not gdn:
        o_ref[0] = y + b_ref[...]
        return
    y = _silu(y)
    q_scale = jnp.where(j < nq, dk ** -0.5, 1.0)
    for h in range(y.shape[1] // dk):
        seg = y[:, h * dk:(h + 1) * dk]
        inv = lax.rsqrt(jnp.sum(seg * seg, axis=-1, keepdims=True) + NORM_EPS)
        o_ref[0, :, h * dk:(h + 1) * dk] = jnp.where(j < 2 * nq, seg * inv * q_scale, seg)


def dwconv(proj, col0, conv0, conv_w, bias=None, gdn_heads=None):
    B, L, _ = proj.shape
    conv_dim = conv0.shape[-1]
    gdn = gdn_heads is not None
    H, dk = gdn_heads if gdn else (1, LANE)
    cw = _tile(H * dk if gdn else conv_dim, 1024, dk)
    assert col0 % cw == 0 and conv_dim % cw == 0
    tl = _tile(L, 256, SUBLANE)
    c0 = jnp.pad(conv0, ((0, 0), (SUBLANE - (CONV_W - 1), 0), (0, 0)))
    rpb, j0 = tl // SUBLANE, col0 // cw
    if bias is None:
        bias = jnp.zeros((conv_dim,), F32)
    return pl.pallas_call(
        functools.partial(_dwconv_kernel, tl=tl, gdn=gdn, dk=dk, nq=H * dk // cw),
        grid=(B, L // tl, conv_dim // cw),
        in_specs=[pl.BlockSpec((1, tl, cw), lambda b, t, j: (b, t, j0 + j)),
                  pl.BlockSpec((1, SUBLANE, cw), lambda b, t, j: (b, jnp.maximum(t * rpb - 1, 0), j0 + j)),
                  pl.BlockSpec((1, SUBLANE, cw), lambda b, t, j: (b, 0, j)),
                  pl.BlockSpec((CONV_W, cw), lambda b, t, j: (0, j)),
                  pl.BlockSpec((1, cw), lambda b, t, j: (0, j))],
        out_specs=pl.BlockSpec((1, tl, cw), lambda b, t, j: (b, t, j)),
        out_shape=jax.ShapeDtypeStruct((B, L, conv_dim), F32),
        scratch_shapes=[pltpu.VMEM((SUBLANE + tl, cw), F32)],
        compiler_params=_params(("parallel", "parallel", "parallel")),
    )(proj, proj, c0, conv_w, bias.reshape(1, conv_dim))


def gdn_core(qkv, proj, gcs, beta, S0, norm_w, c):
    B, L, _ = qkv.shape
    H, dk, dv = S0.shape[1:]
    hs = 4
    nst = min(H // hs, 4)
    hb = hs * nst
    nc = L // c
    R = hs * c

    def stacked(t):
        return jnp.transpose(t.reshape(B, nc, c, H // hs, hs), (0, 3, 1, 4, 2)).reshape(B, H // hs, nc, R)

    gb = jnp.stack([stacked(gcs), stacked(beta)], axis=3)
    ng = H // hb
    assert (2 * H * dk) % (hb * dv) == 0
    v0 = 2 * H * dk // (hb * dv)
    q_spec = pl.BlockSpec((1, c, hb * dk), lambda b, g, n: (b, n, g))
    k_spec = pl.BlockSpec((1, c, hb * dk), lambda b, g, n: (b, n, ng + g))
    vin_spec = pl.BlockSpec((1, c, hb * dv), lambda b, g, n: (b, n, v0 + g))
    z_spec = pl.BlockSpec((1, c, hb * dv), lambda b, g, n: (b, n, v0 + ng + g))
    y_spec = pl.BlockSpec((1, c, hb * dv), lambda b, g, n: (b, n, g))
    gb_spec = pl.BlockSpec((1, nst, 1, 2, R), lambda b, g, n: (b, g, n, 0, 0))
    s_spec = pl.BlockSpec((1, hb, dk, dv), lambda b, g, n: (b, g, 0, 0))
    return pl.pallas_call(
        functools.partial(_gdn_kernel, nst=nst, hs=hs, c=c, dk=dk, dv=dv),
        grid=(B, ng, nc),
        in_specs=[q_spec, k_spec, vin_spec, z_spec, gb_spec, s_spec, pl.BlockSpec((1, dv), lambda b, g, n: (0, 0))],
        out_specs=[y_spec, s_spec],
        out_shape=[jax.ShapeDtypeStruct((B, L, H * dv), BF16), jax.ShapeDtypeStruct((B, H, dk, dv), F32)],
        compiler_params=_params(("parallel", "parallel", "arbitrary")),
    )(qkv, qkv, qkv, proj, gb, S0, norm_w.reshape(1, dv))


def _ret_kernel(q_ref, k_ref, v_ref, gz_ref, gr_ref, s0_ref, gn_ref, y_ref, s_ref, *, hb, c, dk, dv):
    n = pl.program_id(2)

    @pl.when(n == 0)
    def _():
        s_ref[...] = s0_ref[...]

    row = lax.broadcasted_iota(jnp.int32, (c, c), 0)
    col = lax.broadcasted_iota(jnp.int32, (c, c), 1)
    tri = row >= col
    for h in range(hb):
        qh = q_ref[0, :, h * dk:(h + 1) * dk]
        kh = k_ref[0, :, h * dk:(h + 1) * dk] * (dk ** -0.5)
        vh = v_ref[0, :, h * dv:(h + 1) * dv]
        grw = gr_ref[h, 0]
        gcl = jnp.sum(jnp.where(row == col, grw, 0.0), axis=-1, keepdims=True)
        decay = jnp.where(tri, jnp.exp(jnp.where(tri, gcl - grw, 0.0)), 0.0)
        S = s_ref[0, h]
        s = _bdot_nt(qh, kh) * decay
        o = _bdot(s, vh) + _bdot(qh * jnp.exp(gcl), S)
        g_last = gcl[c - 1:c, :]
        s_ref[0, h] = jnp.exp(g_last) * S + _bdot_tn(kh * jnp.exp(g_last - gcl), vh)
        mu = jnp.mean(o, axis=-1, keepdims=True)
        oc = o - mu
        var = jnp.mean(oc * oc, axis=-1, keepdims=True)
        on = oc * lax.rsqrt(var + NORM_EPS) * gn_ref[h]
        gz = gz_ref[0, :, h * dv:(h + 1) * dv]
        y_ref[0, :, h * dv:(h + 1) * dv] = (_silu(gz) * on).astype(y_ref.dtype)


def ret_core(proj, gcs, S0, gn_w, c):
    B, L, _ = proj.shape
    H, dk, dv = S0.shape[1:]
    hb = 4
    nc = L // c
    grow = gcs.T.reshape(H, nc, 1, c)
    ng = H // hb
    assert (2 * H * dk) % (hb * dv) == 0
    v0 = 2 * H * dk // (hb * dv)
    q_spec = pl.BlockSpec((1, c, hb * dk), lambda b, g, n: (b, n, g))
    k_spec = pl.BlockSpec((1, c, hb * dk), lambda b, g, n: (b, n, ng + g))
    vin_spec = pl.BlockSpec((1, c, hb * dv), lambda b, g, n: (b, n, v0 + g))
    gz_spec = pl.BlockSpec((1, c, hb * dv), lambda b, g, n: (b, n, v0 + ng + g))
    v_spec = pl.BlockSpec((1, c, hb * dv), lambda b, g, n: (b, n, g))
    row_spec = pl.BlockSpec((hb, 1, 1, c), lambda b, g, n: (g, n, 0, 0))
    s_spec = pl.BlockSpec((1, hb, dk, dv), lambda b, g, n: (b, g, 0, 0))
    return pl.pallas_call(
        functools.partial(_ret_kernel, hb=hb, c=c, dk=dk, dv=dv),
        grid=(B, H // hb, nc),
        in_specs=[q_spec, k_spec, vin_spec, gz_spec, row_spec, s_spec,
                  pl.BlockSpec((hb, 1, dv), lambda b, g, n: (g, 0, 0))],
        out_specs=[v_spec, s_spec],
        out_shape=[jax.ShapeDtypeStruct((B, L, H * dv), BF16), jax.ShapeDtypeStruct((B, H, dk, dv), F32)],
        compiler_params=_params(("parallel", "parallel", "arbitrary")),
    )(proj, proj, proj, proj, grow, S0, gn_w.reshape(H, 1, dv))


def _lru_kernel(x_ref, gi_ref, wg_ref, bg_ref, sp_ref, h0_ref, y_ref, hl_ref, a_s, b_s, hs_s, h_s, *, tl, seq):
    t = pl.program_id(2)
    x = x_ref[0]
    r = _sigmoid(_bdot(x, wg_ref[0, 0]) + bg_ref[0])
    ig = _sigmoid(_bdot(x, wg_ref[1, 0]) + bg_ref[1])
    log_a = -LRU_C * r * sp_ref[...]
    a = jnp.exp(log_a)
    b = jnp.sqrt(1.0 - jnp.exp(2.0 * log_a)) * (ig * x)
    if seq:
        @pl.when(t == 0)
        def _():
            h_s[...] = h0_ref[0]

        a_s[...] = a
        b_s[...] = b

        def body(i, h):
            h = a_s[pl.ds(i, 1), :] * h + b_s[pl.ds(i, 1), :]
            hs_s[pl.ds(i, 1), :] = h
            return h

        h = lax.fori_loop(0, tl, body, h_s[...], unroll=SUBLANE)
        h_s[...] = h
        hl_ref[0] = h
        hs = hs_s[...]
    else:
        hs = a * h0_ref[0] + b
        hl_ref[0] = hs
    y_ref[0] = (hs * _gelu_tanh(gi_ref[0])).astype(y_ref.dtype)


def lru_core(xc, gate_in, wg2, bg, sp, h0, seq):
    B, L, C = xc.shape
    sb = wg2.shape[-1]
    ns = C // sb
    tl = min(L, 256)
    x_spec = pl.BlockSpec((1, tl, sb), lambda b, s, t: (b, t, s))
    h_rows = 1 if seq else tl
    h_spec = pl.BlockSpec((1, h_rows, sb), (lambda b, s, t: (b, 0, s)) if seq else (lambda b, s, t: (b, t, s)))
    return pl.pallas_call(
        functools.partial(_lru_kernel, tl=tl, seq=seq),
        grid=(B, ns, L // tl),
        in_specs=[x_spec, x_spec,
                  pl.BlockSpec((2, 1, sb, sb), lambda b, s, t: (0, s, 0, 0)),
                  pl.BlockSpec((2, 1, sb), lambda b, s, t: (0, 0, s)),
                  pl.BlockSpec((1, sb), lambda b, s, t: (0, s)),
                  h_spec],
        out_specs=[x_spec, h_spec],
        out_shape=[jax.ShapeDtypeStruct((B, L, C), BF16), jax.ShapeDtypeStruct((B, h_rows if seq else L, C), F32)],
        scratch_shapes=[pltpu.VMEM((tl, sb), F32), pltpu.VMEM((tl, sb), F32), pltpu.VMEM((tl, sb), F32),
                        pltpu.VMEM((1, sb), F32)],
        compiler_params=_params(("parallel", "parallel", "arbitrary")),
    )(xc, gate_in, wg2, bg, sp, h0)


def _moba_prompt_kernel(q_ref, k_ref, v_ref, sl_ref, o_ref, km_s, *, nb, blk, hd):
    qi = pl.program_id(2)

    @pl.when(qi == 0)
    def _():
        km_s[...] = jnp.zeros_like(km_s)
        for n in range(nb):
            km_s[n:n + 1, :] = jnp.mean(k_ref[0, n * blk:(n + 1) * blk, :], axis=0, keepdims=True)

    for qv in range(nb):
        @pl.when(qi == qv)
        def _(qv=qv):
            _moba_query_block(q_ref, k_ref, v_ref, sl_ref, o_ref, km_s, qv, blk, hd)


def _moba_query_block(q_ref, k_ref, v_ref, sl_ref, o_ref, km_s, qv, blk, hd):
    q = q_ref[0] * (hd ** -0.5)
    lane = lax.broadcasted_iota(jnp.int32, (blk, LANE), 1)
    sel = jnp.zeros((blk, LANE), F32)
    if qv > 0:
        gate = jnp.where(lane < qv, _bdot_nt(q, km_s[...]), -jnp.inf)
        for _ in range(min(MOBA_TOPK, qv)):
            gmax = jnp.max(gate, axis=-1, keepdims=True)
            imax = jnp.min(jnp.where(gate == gmax, lane, LANE), axis=-1, keepdims=True)
            sel = jnp.where(lane == imax, 1.0, sel)
            gate = jnp.where(lane == imax, -jnp.inf, gate)
    slope = sl_ref[0]
    rowi = lax.broadcasted_iota(jnp.int32, (blk, blk), 0)
    coli = lax.broadcasted_iota(jnp.int32, (blk, blk), 1)
    s_blocks = []
    for n in range(qv + 1):
        s = _bdot_nt(q, k_ref[0, n * blk:(n + 1) * blk, :])
        dist = (qv - n) * blk + rowi - coli
        ok = (dist >= 0) if n == qv else (sel[:, n:n + 1] > 0.0)
        s_blocks.append(jnp.where(ok, s - slope * dist.astype(F32), NEG_BIG))
    m = s_blocks[0].max(axis=-1, keepdims=True)
    for s in s_blocks[1:]:
        m = jnp.maximum(m, s.max(axis=-1, keepdims=True))
    l = jnp.zeros((blk, 1), F32)
    acc = jnp.zeros((blk, hd), F32)
    for n, s in enumerate(s_blocks):
        p = jnp.where(s > 0.5 * NEG_BIG, jnp.exp(s - m), 0.0)
        l = l + p.sum(axis=-1, keepdims=True)
        acc = acc + _bdot(p, v_ref[0, n * blk:(n + 1) * blk, :])
    o_ref[0] = (acc / l).astype(o_ref.dtype)


def moba_prompt(proj, slopes):
    B, L, D3 = proj.shape
    D = D3 // 3
    H = slopes.shape[0]
    hd = D // H
    blk = MOBA_BLOCK
    nb = L // blk
    q_spec = pl.BlockSpec((1, blk, hd), lambda b, h, i: (b, i, h))
    return pl.pallas_call(
        functools.partial(_moba_prompt_kernel, nb=nb, blk=blk, hd=hd),
        grid=(B, H, nb),
        in_specs=[q_spec,
                  pl.BlockSpec((1, L, hd), lambda b, h, i: (b, 0, H + h)),
                  pl.BlockSpec((1, L, hd), lambda b, h, i: (b, 0, 2 * H + h)),
                  pl.BlockSpec((1, 1, 1), lambda b, h, i: (h, 0, 0))],
        out_specs=q_spec,
        out_shape=jax.ShapeDtypeStruct((B, L, D), BF16),
        scratch_shapes=[pltpu.VMEM((LANE, hd), F32)],
        compiler_params=_params(("parallel", "parallel", "arbitrary")),
    )(proj, proj, proj, slopes.reshape(H, 1, 1))


def _kblock_sum_kernel(pt_ref, k_ref, o_ref, *, ppb):
    p = pl.program_id(1)

    @pl.when(p % ppb == 0)
    def _():
        o_ref[...] = jnp.zeros_like(o_ref)

    o_ref[0, 0] += jnp.sum(k_ref[0, 0], axis=0)


def kblock_sums(cache_k, page_table, layer):
    n_seq, n_pages = page_table.shape
    _, _, page, H, hd = cache_k.shape
    ppb = MOBA_BLOCK // page
    grid_spec = pltpu.PrefetchScalarGridSpec(
        num_scalar_prefetch=1, grid=(n_seq, n_pages),
        in_specs=[pl.BlockSpec((1, 1, page, H, hd), lambda b, p, pt: (layer, pt[b, p], 0, 0, 0))],
        out_specs=pl.BlockSpec((1, 1, H, hd), lambda b, p, pt: (b, p // ppb, 0, 0)))
    return pl.pallas_call(
        functools.partial(_kblock_sum_kernel, ppb=ppb), grid_spec=grid_spec,
        out_shape=jax.ShapeDtypeStruct((n_seq, n_pages // ppb, H, hd), F32),
        compiler_params=_params(("parallel", "arbitrary")),
    )(page_table, cache_k)


def _moba_step_kernel(pt_ref, q_ref, kn_ref, vn_ref, ks_ref, sl_ref, k_ref, v_ref, o_ref, m_s, l_s, acc_s, sel_s,
                      *, page, n_pages, hd):
    p = pl.program_id(1)
    q = q_ref[0] * (hd ** -0.5)

    @pl.when(p == 0)
    def _():
        m_s[...] = jnp.sum(q * kn_ref[0], axis=-1, keepdims=True)
        l_s[...] = jnp.ones_like(l_s)
        acc_s[...] = vn_ref[0]
        gate = jnp.sum(ks_ref[0] * (1.0 / MOBA_BLOCK) * q[None], axis=-1, keepdims=True)
        blk = lax.broadcasted_iota(jnp.int32, gate.shape, 0).astype(F32)
        sel = jnp.zeros_like(gate)
        for _ in range(min(MOBA_TOPK, gate.shape[0])):
            gmax = jnp.max(gate, axis=0, keepdims=True)
            imax = jnp.min(jnp.where(gate == gmax, blk, float(2 * gate.shape[0])), axis=0, keepdims=True)
            sel = jnp.where(blk == imax, 1.0, sel)
            gate = jnp.where(blk == imax, -jnp.inf, gate)
        sel_s[...] = sel

    k = k_ref[0, 0]
    s = jnp.sum(k * q[None], axis=-1, keepdims=True)
    tok = lax.broadcasted_iota(jnp.int32, s.shape, 0)
    dist = (n_pages * page - (p * page + tok)).astype(F32)
    ok = sel_s[p // (MOBA_BLOCK // page)][None] > 0.0
    s = jnp.where(ok, s - sl_ref[...][None] * dist, NEG_BIG)
    m_old = m_s[...]
    m_new = jnp.maximum(m_old, jnp.max(s, axis=0))
    alpha = jnp.exp(m_old - m_new)
    pr = jnp.where(ok, jnp.exp(s - m_new[None]), 0.0)
    l_s[...] = alpha * l_s[...] + jnp.sum(pr, axis=0)
    acc_s[...] = alpha * acc_s[...] + jnp.sum(pr * v_ref[0, 0], axis=0)
    m_s[...] = m_new

    @pl.when(p == n_pages - 1)
    def _():
        o_ref[0] = (acc_s[...] / l_s[...]).astype(o_ref.dtype)


def moba_step(q, k_new, v_new, k_sums, slopes, cache_k, cache_v, page_table, layer):
    n_seq, n_pages = page_table.shape
    _, _, page, H, hd = cache_k.shape
    n_blk = k_sums.shape[1]
    assert n_blk * MOBA_BLOCK == n_pages * page
    vec = pl.BlockSpec((1, H, hd), lambda b, p, pt: (b, 0, 0))
    kv = pl.BlockSpec((1, 1, page, H, hd), lambda b, p, pt: (layer, pt[b, p], 0, 0, 0))
    grid_spec = pltpu.PrefetchScalarGridSpec(
        num_scalar_prefetch=1, grid=(n_seq, n_pages),
        in_specs=[vec, vec, vec,
                  pl.BlockSpec((1, n_blk, H, hd), lambda b, p, pt: (b, 0, 0, 0)),
                  pl.BlockSpec((H, 1), lambda b, p, pt: (0, 0)),
                  kv, kv],
        out_specs=vec,
        scratch_shapes=[pltpu.VMEM((H, 1), F32), pltpu.VMEM((H, 1), F32), pltpu.VMEM((H, hd), F32),
                        pltpu.VMEM((n_blk, H, 1), F32)])
    return pl.pallas_call(
        functools.partial(_moba_step_kernel, page=page, n_pages=n_pages, hd=hd), grid_spec=grid_spec,
        out_shape=jax.ShapeDtypeStruct((n_seq, H, hd), BF16),
        compiler_params=_params(("parallel", "arbitrary")),
    )(page_table, q, k_new, v_new, k_sums, slopes.reshape(H, 1), cache_k, cache_v)


def _top_rows(s, pos, k):
    vals, idxs = [], []
    for _ in range(k):
        mx = jnp.max(s, axis=0, keepdims=True)
        ix = jnp.min(jnp.where(s == mx, pos, 1e9), axis=0, keepdims=True)
        s = jnp.where(pos == ix, -jnp.inf, s)
        vals.append(mx)
        idxs.append(ix)
    return vals, idxs


def _peer_route_kernel(q_ref, keys_ref, g_ref, r_ref, c_ref, *, dsub, tt, hp):
    for hh in range(hp):
        _route_head(q_ref, keys_ref, g_ref, r_ref, c_ref, hh, dsub, tt)


def _route_head(q_ref, keys_ref, g_ref, r_ref, c_ref, hh, dsub, tt):
    key_pos = lax.broadcasted_iota(jnp.int32, (N_KEYS, tt), 0).astype(F32)
    s0 = _bdot_nt(keys_ref[2 * hh], q_ref[:, 2 * hh * dsub:(2 * hh + 1) * dsub])
    s1 = _bdot_nt(keys_ref[2 * hh + 1], q_ref[:, (2 * hh + 1) * dsub:(2 * hh + 2) * dsub])
    v0, i0 = _top_rows(s0, key_pos, P_TOPK)
    v1, i1 = _top_rows(s1, key_pos, P_TOPK)
    top_s1 = jnp.concatenate(v1, axis=0)
    top_i0 = jnp.concatenate(i0, axis=0)
    top_i1 = jnp.concatenate(i1, axis=0)
    assert P_TOPK == 2 * SUBLANE
    sub = lax.broadcasted_iota(jnp.int32, (SUBLANE, tt), 0).astype(F32)
    blocks = [v0[0] + top_s1]
    poss = [lax.broadcasted_iota(jnp.int32, (P_TOPK, tt), 0).astype(F32)]
    for a in range(1, SUBLANE):
        blk = v0[a] + top_s1[:SUBLANE]
        blocks.append(jnp.where(sub < float(P_TOPK // (a + 1)), blk, -jnp.inf))
        poss.append(sub + float(a * P_TOPK))
    blocks.append(jnp.concatenate(v0[SUBLANE:], axis=0) + v1[0])
    poss.append((sub + float(SUBLANE)) * float(P_TOPK))
    best_s, best_pos = _top_rows(jnp.concatenate(blocks, axis=0), jnp.concatenate(poss, axis=0), P_TOPK)
    slot = lax.broadcasted_iota(jnp.int32, (P_TOPK, tt), 0).astype(F32)
    e = [jnp.exp(b - best_s[0]) for b in best_s]
    z = e[0]
    for x in e[1:]:
        z = z + x
    for r in range(P_TOPK):
        a = jnp.floor(best_pos[r] * (1.0 / P_TOPK))
        b = best_pos[r] - a * P_TOPK
        o = hh * P_TOPK + r
        g_ref[o:o + 1, :] = e[r] / z
        r_ref[o:o + 1, :] = jnp.sum(jnp.where(slot == a, top_i0, 0.0), axis=0, keepdims=True)
        c_ref[o:o + 1, :] = jnp.sum(jnp.where(slot == b, top_i1, 0.0), axis=0, keepdims=True)


def peer_route(q, keys):
    T = q.shape[0]
    dsub = keys.shape[-1]
    n_heads = keys.shape[0] // 2
    tt = LANE
    hp = 2 if n_heads % 2 == 0 else 1
    out_spec = pl.BlockSpec((hp * P_TOPK, tt), lambda i, h: (h, i))
    shape = jax.ShapeDtypeStruct((n_heads * P_TOPK, T), F32)
    return pl.pallas_call(
        functools.partial(_peer_route_kernel, dsub=dsub, tt=tt, hp=hp),
        grid=(T // tt, n_heads // hp),
        in_specs=[pl.BlockSpec((tt, hp * 2 * dsub), lambda i, h: (i, h)),
                  pl.BlockSpec((hp * 2, N_KEYS, dsub), lambda i, h: (h, 0, 0))],
        out_specs=[out_spec, out_spec, out_spec], out_shape=[shape, shape, shape],
        compiler_params=_params(("parallel", "parallel")),
    )(q, keys)


def _peer_gates_kernel(g_ref, i_ref, j_ref, o_ref, m_s, *, tb):
    iota = lax.broadcasted_iota(jnp.int32, (tb, N_KEYS, LANE), 1).astype(F32)
    a = jnp.where(i_ref[...] == iota, g_ref[...], 0.0)
    a_hi = a.astype(BF16)
    a_lo = (a - a_hi.astype(F32)).astype(BF16)
    bt = jnp.where(j_ref[...] == iota, 1.0, 0.0).astype(BF16)
    dn = (((2,), (2,)), ((0,), (0,)))
    both = lax.dot_general(jnp.concatenate([a_hi, a_lo], axis=1), bt, dn, preferred_element_type=F32)
    m_s[...] = both[:, :N_KEYS] + both[:, N_KEYS:]
    low = [(lax.broadcasted_iota(jnp.int32, (SUBLANE, LANE), 0) & s) == 0 for s in (4, 2, 1)]
    for tg in range(tb // SUBLANE):
        for a in range(N_KEYS // SUBLANE):
            v = [m_s[tg * SUBLANE + r, a * SUBLANE:(a + 1) * SUBLANE, :] for r in range(SUBLANE)]
            for keep, s in zip(low, (4, 2, 1)):
                for k in range(SUBLANE):
                    if k & s == 0:
                        x, y = v[k], v[k + s]
                        v[k] = jnp.where(keep, x, pltpu.roll(y, s, 0))
                        v[k + s] = jnp.where(keep, pltpu.roll(x, SUBLANE - s, 0), y)
            for c in range(SUBLANE):
                i = a * SUBLANE + c
                o_ref[tg * SUBLANE:(tg + 1) * SUBLANE, i * N_KEYS:(i + 1) * N_KEYS] = v[c]


def peer_gate_matrix(gates, rows, cols):
    T, slots = gates.shape
    assert slots == LANE
    tb = min(T, 32)
    spec = pl.BlockSpec((tb, 1, slots), lambda i: (i, 0, 0))
    return pl.pallas_call(
        functools.partial(_peer_gates_kernel, tb=tb),
        grid=(T // tb,), in_specs=[spec, spec, spec],
        out_specs=pl.BlockSpec((tb, N_KEYS * N_KEYS), lambda i: (i, 0)),
        out_shape=jax.ShapeDtypeStruct((T, N_KEYS * N_KEYS), F32),
        scratch_shapes=[pltpu.VMEM((tb, N_KEYS, N_KEYS), F32)],
        compiler_params=_params(("parallel",)),
    )(gates.reshape(T, 1, slots), rows.reshape(T, 1, slots), cols.reshape(T, 1, slots))


def _peer_experts_kernel(x_ref, g_ref, u_ref, v_ref, o_ref):
    e = pl.program_id(1)

    @pl.when(e == 0)
    def _():
        o_ref[...] = jnp.zeros_like(o_ref)

    act = _gelu_tanh(_bdot_nt(x_ref[...], u_ref[...]))
    o_ref[...] += _bdot(g_ref[...] * act, v_ref[...])


def peer_experts(x, gmat, u, v, layer):
    T, D = x.shape
    E = u.shape[1]
    tm = min(T, 512)
    te = 512
    w_spec = pl.BlockSpec((None, te, D), lambda i, e: (layer, e, 0))
    return pl.pallas_call(
        _peer_experts_kernel,
        grid=(T // tm, E // te),
        in_specs=[pl.BlockSpec((tm, D), lambda i, e: (i, 0)), pl.BlockSpec((tm, te), lambda i, e: (i, e)),
                  w_spec, w_spec],
        out_specs=pl.BlockSpec((tm, D), lambda i, e: (i, 0)),
        out_shape=jax.ShapeDtypeStruct((T, D), F32),
        compiler_params=_params(("parallel", "arbitrary")),
    )(x, gmat, u, v)


def peer_ffn(hs, layer, wq, keys, u, v):
    n_heads = keys.shape[0]
    dsub = keys.shape[-1]
    outs = []
    for h, q in zip(hs, proj_pair(hs, wq, layer=layer)):
        B, L, D = h.shape
        T = B * L
        q = jnp.pad(q.reshape(T, -1), ((0, -T % LANE), (0, 0)))
        gates, rows, cols = (t[:, :T].T for t in peer_route(q, keys.reshape(n_heads * 2, N_KEYS, dsub)))
        outs.append(peer_experts(h.reshape(T, D), peer_gate_matrix(gates, rows, cols), u, v, layer).reshape(B, L, D))
    return outs


def _pad_rows(t, rows):
    return jnp.pad(t, ((0, 0), (0, rows - t.shape[1])) + ((0, 0),) * (t.ndim - 2))


def _chunk_cumsum(g, c):
    B, L, H = g.shape
    return jnp.cumsum(g.reshape(B, L // c, c, H), axis=2).reshape(B, L, H)


def gdn_mixer(hs, states, params):
    w_in, w_out = params[0], params[-1]
    H, dk, dv = states[0][0].shape[1:]
    n_main = 2 * H * dk + 2 * H * dv
    projs = proj_pair(hs, w_in, n_cols=n_main)
    bas = proj_pair(hs, w_in[:, n_main:])
    ys, new_states = zip(*(_gdn_group(projs[g], bas[g], states[g], params) for g in range(2)))
    return proj_pair(ys, w_out), new_states


def _gdn_group(proj, ba, state, params):
    S0, conv0 = state
    _, conv_w, a_log, dt_bias, norm_w, _ = params
    B, L, _ = proj.shape
    H, dk, dv = S0.shape[1:]
    conv_dim = 2 * H * dk + H * dv
    conv_new = jnp.concatenate([conv0, proj[:, max(L - (CONV_W - 1), 0):, :conv_dim]], axis=1)[:, -(CONV_W - 1):]
    beta = jax.nn.sigmoid(ba[..., :H])
    g = -jnp.exp(a_log) * jax.nn.softplus(ba[..., H:] + dt_bias)
    c = math.gcd(L, GDN_CHUNK)
    if c < SUBLANE:
        c = SUBLANE
        proj, g, beta = (_pad_rows(t, c) for t in (proj, g, beta))
    qkv = dwconv(proj, 0, conv0, conv_w, gdn_heads=(H, dk))
    y, S = gdn_core(qkv, proj, _chunk_cumsum(g, c), beta, S0, norm_w, c)
    return y[:, :L], (S, conv_new)


def lru_mixer(hs, states, params):
    w_in, w_out = params[0], params[-1]
    projs = proj_pair(hs, w_in)
    ys, new_states = zip(*(_lru_group(projs[g], states[g], params) for g in range(2)))
    return proj_pair(ys, w_out), new_states


def _lru_group(proj, state, params):
    h0, conv0 = state
    _, conv_w, conv_b, w_gates, b_gates, lam, _ = params
    B, L, _ = proj.shape
    C = h0.shape[-1]
    conv_new = jnp.concatenate([conv0, proj[:, max(L - (CONV_W - 1), 0):, C:]], axis=1)[:, -(CONV_W - 1):]
    xc = dwconv(_pad_rows(proj, max(L, SUBLANE)), C, conv0, conv_w, bias=conv_b)[:, :L]
    nb, bw = w_gates.shape[1], w_gates.shape[2]
    per = (bw * LANE // math.gcd(bw, LANE)) // bw
    wg = w_gates.reshape(2, nb // per, per, bw, bw)
    eye = jnp.eye(per, dtype=w_gates.dtype)
    wg2 = jnp.einsum('gspij,pq->gspiqj', wg, eye).reshape(2, nb // per, per * bw, per * bw)
    sp = jax.nn.softplus(-lam).reshape(1, C)
    bg = b_gates.reshape(2, 1, C)
    if L > 1:
        y, hl = lru_core(xc, proj, wg2, bg, sp, h0.reshape(B, 1, C), True)
        h_last = hl.reshape(B, C)
    else:
        y, hl = lru_core(xc.reshape(1, B, C), proj.reshape(1, B, 2 * C), wg2, bg, sp, h0.reshape(1, B, C), False)
        y = y.reshape(B, 1, C)
        h_last = hl.reshape(B, C)
    return y, (h_last, conv_new)


def ret_mixer(hs, states, params):
    w_in, gn_w, w_out = params
    projs = proj_pair(hs, w_in)
    ys, new_states = zip(*(_ret_group(projs[g], states[g], gn_w) for g in range(2)))
    return proj_pair(ys, w_out), new_states


def _ret_group(proj, state, gn_w):
    (S0,) = state
    B, L, _ = proj.shape
    H = S0.shape[1]
    log_gamma = jnp.log1p(-jnp.exp2(-5.0 - jnp.arange(H, dtype=F32)))
    c = math.gcd(L, RET_CHUNK)
    Lp = L
    if c < SUBLANE:
        Lp = c = SUBLANE
        proj = _pad_rows(proj, Lp)
    pos = jnp.arange(Lp)
    steps = jnp.minimum(pos % c + 1, jnp.maximum(L - (pos // c) * c, 0)).astype(F32)
    y, S = ret_core(proj, steps[:, None] * log_gamma[None, :], S0, gn_w, c)
    return y[:, :L], (S,)


def moba_mixer(hs, pasts, params, n_heads):
    w_in, w_out = params
    projs = proj_pair(hs, w_in)
    ys, new_states = zip(*(_moba_group(projs[g], pasts[g], n_heads) for g in range(2)))
    return proj_pair(ys, w_out), new_states


def _moba_group(proj, past, n_heads):
    B, L, D3 = proj.shape
    D = D3 // 3
    hd = D // n_heads
    q, k, v = proj[..., :D], proj[..., D:2 * D], proj[..., 2 * D:]
    slopes = jnp.exp2(-8.0 * (jnp.arange(n_heads, dtype=F32) + 1.0) / n_heads)
    if past is None:
        o = moba_prompt(proj, slopes)
    else:
        cache_k, cache_v, page_table, layer = past
        k_sums = kblock_sums(cache_k, page_table, layer)
        o = moba_step(q.reshape(B, n_heads, hd), k.reshape(B, n_heads, hd), v.reshape(B, n_heads, hd), k_sums,
                      slopes, cache_k, cache_v, page_table, layer).reshape(B, L, D)
    return o, (k.reshape(B, L, n_heads, hd), v.reshape(B, L, n_heads, hd))


def kernel(x_prompt, x_sample, c_prompt, c_sample, state_gdn_S, state_gdn_conv, state_lru_h, state_lru_conv, state_ret_S, cache_k, cache_v, page_table, w_ada, b_ada, ln_g, ln_b, peer_wq, peer_keys, peer_u, peer_v, gdn_w_in, gdn_conv_w, gdn_a_log, gdn_dt_bias, gdn_norm_w, gdn_w_out, lru_w_in, lru_conv_w, lru_conv_b, lru_w_gates, lru_b_gates, lru_lambda, lru_w_out, ret_w_in, ret_gn_w, ret_w_out, moba_w_in, moba_w_out):
    depth, D = w_ada.shape[0], w_ada.shape[1]
    n_mixers = 4
    Bp, Bs = x_prompt.shape[0], x_sample.shape[0]
    H_A, DK_A, DV_A = state_gdn_S.shape[2:]
    conv_dim_a = state_gdn_conv.shape[-1]
    C_B = state_lru_h.shape[-1]
    H_C, DK_C, DV_C = state_ret_S.shape[2:]
    H_D = cache_k.shape[3]

    c_all = jax.nn.silu(jnp.concatenate([c_prompt, c_sample], axis=0))
    n_c = c_all.shape[0]
    c_pad = jnp.pad(c_all, ((0, -n_c % SUBLANE), (0, 0)))
    xs = [x_prompt, x_sample]
    groups = ((0, Bp), (Bp, Bs))
    new_states = {(m, grp): [] for m in range(n_mixers) for grp in range(2)}
    hs = [None, None]
    mod_all = [matmul(c_pad, w_ada, i, bias=b_ada[i]) for i in range(depth)]
    u_bf, v_bf = peer_u.astype(BF16), peer_v.astype(BF16)
    mods = [[[mod[s:s + n, None, r * D:(r + 1) * D] for r in range(6)] for s, n in groups] for mod in mod_all]
    for i in range(depth):
        m, j = i % n_mixers, i // n_mixers
        if i == 0:
            hs = [modulate(xs[g], mods[0][g][1], mods[0][g][0]) for g in range(2)]
        if m == 0:
            params = (gdn_w_in[j], gdn_conv_w[j], gdn_a_log[j], gdn_dt_bias[j], gdn_norm_w[j], gdn_w_out[j])
            states = ((jnp.zeros((Bp, H_A, DK_A, DV_A), F32), jnp.zeros((Bp, CONV_W - 1, conv_dim_a), F32)),
                      (state_gdn_S[j], state_gdn_conv[j]))
            ys, ns = gdn_mixer(hs, states, params)
        elif m == 1:
            params = (lru_w_in[j], lru_conv_w[j], lru_conv_b[j], lru_w_gates[j], lru_b_gates[j], lru_lambda[j],
                      lru_w_out[j])
            states = ((jnp.zeros((Bp, C_B), F32), jnp.zeros((Bp, CONV_W - 1, C_B), F32)),
                      (state_lru_h[j], state_lru_conv[j]))
            ys, ns = lru_mixer(hs, states, params)
        elif m == 2:
            states = ((jnp.zeros((Bp, H_C, DK_C, DV_C), F32),), (state_ret_S[j],))
            ys, ns = ret_mixer(hs, states, (ret_w_in[j], ret_gn_w[j], ret_w_out[j]))
        else:
            ys, ns = moba_mixer(hs, (None, (cache_k, cache_v, page_table, j)), (moba_w_in[j], moba_w_out[j]), H_D)
        x1, h2 = [], []
        for g in range(2):
            new_states[(m, g)].append(ns[g])
            _, _, g1, sh2, sc2, _ = mods[i][g]
            xg, hg = ln_residual(xs[g], ys[g], g1, sc2, sh2, ln_g[i, 0], ln_b[i, 0])
            x1.append(xg)
            h2.append(hg)
        y2 = peer_ffn(h2, i, peer_wq, peer_keys[i], u_bf, v_bf)
        for g in range(2):
            if i + 1 < depth:
                sh_n, sc_n = mods[i + 1][g][0], mods[i + 1][g][1]
            else:
                sh_n = sc_n = jnp.zeros((xs[g].shape[0], 1, D), F32)
            xs[g], hs[g] = ln_residual(x1[g], y2[g], mods[i][g][5], sc_n, sh_n, ln_g[i, 1], ln_b[i, 1])

    def stacked(m, grp):
        return [jnp.stack(parts) for parts in zip(*new_states[(m, grp)])]

    gdn_S_p, gdn_conv_p = stacked(0, 0)
    gdn_S_s, gdn_conv_s = stacked(0, 1)
    lru_h_p, lru_conv_p = stacked(1, 0)
    lru_h_s, lru_conv_s = stacked(1, 1)
    (ret_S_p,) = stacked(2, 0)
    (ret_S_s,) = stacked(2, 1)
    k_p, v_p = stacked(3, 0)
    k_s, v_s = stacked(3, 1)
    return (xs[0], xs[1], gdn_S_p, gdn_S_s, gdn_conv_p, gdn_conv_s, lru_h_p, lru_h_s, lru_conv_p, lru_conv_s,
            ret_S_p, ret_S_s, k_p, k_s, v_p, v_s)
```

```python
import functools
import math

import jax
import jax.numpy as jnp
from jax import lax
from jax.experimental import pallas as pl
from jax.experimental.pallas import tpu as pltpu

F32 = jnp.float32
BF16 = jnp.bfloat16

DEPTH = 4
ALPHA = (2.0 * DEPTH) ** 0.25
CONV_W = 4
LN_EPS = 1e-5
NORM_EPS = 1e-6
GDN_CHUNK = 64
RET_CHUNK = 64
LRU_C = 8.0
MOBA_BLOCK = 256
MOBA_TOPK = 3
P_TOPK = 16
N_KEYS = 128
LANE = 128
SUBLANE = 8
VMEM_LIMIT = 56 * 1024 * 1024
NEG_BIG = -1e30

_NT = (((1,), (1,)), ((), ()))
_TN = (((0,), (0,)), ((), ()))


def _tile(n, cap, align):
    for t in range(min(cap, n) // align * align, 0, -align):
        if n % t == 0:
            return t
    return n


def _params(sem):
    return pltpu.CompilerParams(dimension_semantics=sem, vmem_limit_bytes=VMEM_LIMIT)


def _bdot(a, b):
    return jnp.dot(a.astype(BF16), b.astype(BF16), preferred_element_type=F32)


def _bdot_nt(a, b):
    return lax.dot_general(a.astype(BF16), b.astype(BF16), _NT, preferred_element_type=F32)


def _bdot_tn(a, b):
    return lax.dot_general(a.astype(BF16), b.astype(BF16), _TN, preferred_element_type=F32)


def _fdot(a, b):
    return jnp.dot(a, b, preferred_element_type=F32, precision=lax.Precision.HIGHEST)


def _gelu_tanh(x):
    return 0.5 * x * (1.0 + jnp.tanh(math.sqrt(2.0 / math.pi) * (x + 0.044715 * (x * x * x))))


def _sigmoid(x):
    return 1.0 / (1.0 + jnp.exp(-x))


def _silu(x):
    return x * _sigmoid(x)


def _mm_kernel(a_ref, b_ref, *rest, nk, has_bias):
    if has_bias:
        bias_ref, o_ref, acc_ref = rest
    else:
        o_ref, acc_ref = rest
    k = pl.program_id(2)

    @pl.when(k == 0)
    def _():
        acc_ref[...] = jnp.zeros_like(acc_ref)

    acc_ref[...] += _bdot(a_ref[...], b_ref[...])

    @pl.when(k == nk - 1)
    def _():
        r = acc_ref[...]
        if has_bias:
            r = r + bias_ref[...]
        o_ref[...] = r.astype(o_ref.dtype)


def matmul(a, b, layer, *, bias=None, out_dtype=F32):
    M, K = a.shape
    N = b.shape[2]
    tm = _tile(M, 1024, SUBLANE)
    tn = _tile(N, 1024 if M > 64 else 2048, LANE)
    tk = _tile(K, 512, LANE)
    nk = K // tk
    in_specs = [pl.BlockSpec((tm, tk), lambda i, j, k: (i, k)),
                pl.BlockSpec((None, tk, tn), lambda i, j, k: (layer, k, j))]
    args = [a, b]
    if bias is not None:
        in_specs.append(pl.BlockSpec((1, tn), lambda i, j, k: (0, j)))
        args.append(bias.reshape(1, -1))
    return pl.pallas_call(
        functools.partial(_mm_kernel, nk=nk, has_bias=bias is not None),
        grid=(M // tm, N // tn, nk),
        in_specs=in_specs,
        out_specs=pl.BlockSpec((tm, tn), lambda i, j, k: (i, j)),
        out_shape=jax.ShapeDtypeStruct((M, N), out_dtype),
        scratch_shapes=[pltpu.VMEM((tm, tn), F32)],
        compiler_params=_params(("parallel", "parallel", "arbitrary")),
    )(*args)


def _mm_pair_kernel(ap_ref, as_ref, b_ref, op_ref, os_ref, b_s):
    @pl.when(pl.program_id(1) == 0)
    def _():
        b_s[...] = b_ref[...].astype(BF16)
        os_ref[...] = jnp.dot(as_ref[...].astype(BF16), b_s[...], preferred_element_type=F32)

    op_ref[...] = jnp.dot(ap_ref[...].astype(BF16), b_s[...], preferred_element_type=F32)


def matmul_pair(a_p, a_s, b, n_cols=None, layer=None):
    Mp, K = a_p.shape
    Ms = a_s.shape[0]
    N = b.shape[-1] if n_cols is None else n_cols
    wide = K <= 5120
    tm = _tile(Mp, 1024 if wide else 512, SUBLANE)
    tn = _tile(N, 512 if wide else 256, LANE)
    if layer is None:
        b_spec = pl.BlockSpec((K, tn), lambda j, i: (0, j))
    else:
        b_spec = pl.BlockSpec((None, K, tn), lambda j, i: (layer, 0, j))
    return pl.pallas_call(
        _mm_pair_kernel,
        grid=(N // tn, Mp // tm),
        in_specs=[pl.BlockSpec((tm, K), lambda j, i: (i, 0)), pl.BlockSpec((Ms, K), lambda j, i: (0, 0)), b_spec],
        out_specs=[pl.BlockSpec((tm, tn), lambda j, i: (i, j)), pl.BlockSpec((Ms, tn), lambda j, i: (0, j))],
        out_shape=[jax.ShapeDtypeStruct((Mp, N), F32), jax.ShapeDtypeStruct((Ms, N), F32)],
        scratch_shapes=[pltpu.VMEM((K, tn), BF16)],
        compiler_params=_params(("parallel", "arbitrary")),
    )(a_p, a_s, b)


def proj_pair(xs, w, n_cols=None, layer=None):
    (Bp, Lp, K), (Bs, Ls, _) = xs[0].shape, xs[1].shape
    o_p, o_s = matmul_pair(xs[0].reshape(Bp * Lp, K), xs[1].reshape(Bs * Ls, K), w, n_cols, layer)
    return o_p.reshape(Bp, Lp, -1), o_s.reshape(Bs, Ls, -1)


def _modulate_kernel(x_ref, sc_ref, sh_ref, h_ref):
    h_ref[0] = (x_ref[0] * (1.0 + sc_ref[0]) + sh_ref[0]).astype(h_ref.dtype)


def modulate(x, sc, sh):
    B, L, D = x.shape
    tl = min(L, 256)
    row = pl.BlockSpec((1, tl, D), lambda b, t: (b, t, 0))
    vec = pl.BlockSpec((1, 1, D), lambda b, t: (b, 0, 0))
    return pl.pallas_call(
        _modulate_kernel, grid=(B, L // tl), in_specs=[row, vec, vec], out_specs=row,
        out_shape=jax.ShapeDtypeStruct((B, L, D), BF16),
        compiler_params=_params(("parallel", "parallel")),
    )(x, sc, sh)


def _ln_kernel(x_ref, y_ref, g_ref, sc_ref, sh_ref, lg_ref, lb_ref, xo_ref, ho_ref):
    z = ALPHA * x_ref[0] + (1.0 + g_ref[0]) * y_ref[0]
    mu = jnp.mean(z, axis=-1, keepdims=True)
    zc = z - mu
    var = jnp.mean(zc * zc, axis=-1, keepdims=True)
    xn = zc * lax.rsqrt(var + LN_EPS) * lg_ref[...] + lb_ref[...]
    xo_ref[0] = xn
    ho_ref[0] = (xn * (1.0 + sc_ref[0]) + sh_ref[0]).astype(ho_ref.dtype)


def ln_residual(x, y, gate, sc_next, sh_next, ln_g, ln_b):
    B, L, D = x.shape
    tl = min(L, 256)
    row = pl.BlockSpec((1, tl, D), lambda b, t: (b, t, 0))
    vec = pl.BlockSpec((1, 1, D), lambda b, t: (b, 0, 0))
    par = pl.BlockSpec((1, D), lambda b, t: (0, 0))
    return pl.pallas_call(
        _ln_kernel, grid=(B, L // tl),
        in_specs=[row, row, vec, vec, vec, par, par], out_specs=[row, row],
        out_shape=[jax.ShapeDtypeStruct((B, L, D), F32), jax.ShapeDtypeStruct((B, L, D), BF16)],
        compiler_params=_params(("parallel", "parallel")),
    )(x, y, gate, sc_next, sh_next, ln_g.reshape(1, D), ln_b.reshape(1, D))


def _split_bf16(a):
    hi = a.astype(BF16)
    return hi, (a - hi.astype(F32)).astype(BF16)


def _dot3(a, b):
    ah, al = _split_bf16(a)
    bh, bl = _split_bf16(b)
    dot = functools.partial(jnp.dot, preferred_element_type=F32)
    return dot(ah, bh) + (dot(ah, bl) + dot(al, bh))


def _head_blocks(x, row_head, n):
    return jnp.concatenate([jnp.where(row_head == j, x, 0.0) for j in range(n)], axis=1)


def _gdn_kernel(q_ref, k_ref, v_ref, z_ref, gb_ref, s0_ref, nw_ref, y_ref, s_ref, *, nst, hs, c, dk, dv):
    n = pl.program_id(2)

    @pl.when(n == 0)
    def _():
        s_ref[...] = s0_ref[...]

    R = hs * c
    shift = int(math.log2(c))
    row = lax.broadcasted_iota(jnp.int32, (R, R), 0)
    col = lax.broadcasted_iota(jnp.int32, (R, R), 1)
    same = jnp.right_shift(row, shift) == jnp.right_shift(col, shift)
    tri = same & (row >= col)
    strict = same & (row > col)
    diag = row == col
    eye = diag.astype(F32)
    row_head = jnp.right_shift(lax.broadcasted_iota(jnp.int32, (R, 1), 0), shift)

    def as_column(x_row):
        return jnp.sum(jnp.where(diag, x_row, 0.0), axis=-1, keepdims=True)

    for st in range(nst):
        heads = range(st * hs, (st + 1) * hs)
        q = jnp.concatenate([q_ref[0, :, h * dk:(h + 1) * dk] for h in heads], axis=0)
        k = jnp.concatenate([k_ref[0, :, h * dk:(h + 1) * dk] for h in heads], axis=0)
        v = jnp.concatenate([v_ref[0, :, h * dv:(h + 1) * dv] for h in heads], axis=0)
        z = jnp.concatenate([z_ref[0, :, h * dv:(h + 1) * dv] for h in heads], axis=0)
        grw = gb_ref[0, st, 0, 0:1, :]
        gcl = as_column(grw)
        bt = as_column(gb_ref[0, st, 0, 1:2, :])
        g_ends = [grw[:, (j + 1) * c - 1:(j + 1) * c] for j in range(hs)]
        g_end = g_ends[0]
        for j in range(1, hs):
            g_end = jnp.where(row_head == j, g_ends[j], g_end)
        decay = jnp.where(tri, jnp.exp(jnp.where(tri, gcl - grw, 0.0)), 0.0)
        kb = k * bt
        m = jnp.where(strict, _bdot_nt(kb, k) * decay, 0.0)
        tinv = eye - m
        p = m
        for _ in range(shift - 1):
            p = _dot3(p, p)
            tinv = tinv + _dot3(tinv, p)
        u = _dot3(tinv, v * bt)
        w = _dot3(tinv, kb * jnp.exp(gcl))
        qk = jnp.where(tri, _bdot_nt(q, k) * decay, 0.0)
        S = jnp.concatenate([s_ref[0, h] for h in heads], axis=0)
        v_new = u - _bdot(_head_blocks(w, row_head, hs), S)
        o = _bdot(_head_blocks(q * jnp.exp(gcl), row_head, hs), S) + _bdot(qk, v_new)
        s_add = _bdot_tn(_head_blocks(k * jnp.exp(g_end - gcl), row_head, hs), v_new)
        o = o * lax.rsqrt(jnp.mean(o * o, axis=-1, keepdims=True) + NORM_EPS) * nw_ref[...]
        y = (o * _silu(z)).astype(y_ref.dtype)
        for j, h in enumerate(heads):
            s_ref[0, h] = S[j * dk:(j + 1) * dk] * jnp.exp(g_ends[j]) + s_add[j * dk:(j + 1) * dk]
            y_ref[0, :, h * dv:(h + 1) * dv] = y[j * c:(j + 1) * c]


def _dwconv_kernel(x_ref, prev_ref, c0_ref, w_ref, b_ref, o_ref, xs, *, tl, gdn, dk, nq):
    t, j = pl.program_id(1), pl.program_id(2)
    xs[0:SUBLANE, :] = jnp.where(t == 0, c0_ref[0], prev_ref[0])
    xs[SUBLANE:SUBLANE + tl, :] = x_ref[0]
    first = SUBLANE - (CONV_W - 1)
    y = xs[first:first + tl, :] * w_ref[0:1, :]
    for r in range(1, CONV_W):
        y = y + xs[first + r:first + r + tl, :] * w_ref[r:r + 1, :]
    if not gdn:
        o_ref[0] = y + b_ref[...]
        return
    y = _silu(y)
    q_scale = jnp.where(j < nq, dk ** -0.5, 1.0)
    for h in range(y.shape[1] // dk):
        seg = y[:, h * dk:(h + 1) * dk]
        inv = lax.rsqrt(jnp.sum(seg * seg, axis=-1, keepdims=True) + NORM_EPS)
        o_ref[0, :, h * dk:(h + 1) * dk] = jnp.where(j < 2 * nq, seg * inv * q_scale, seg)


def dwconv(proj, col0, conv0, conv_w, bias=None, gdn_heads=None):
    B, L, _ = proj.shape
    conv_dim = conv0.shape[-1]
    gdn = gdn_heads is not None
    H, dk = gdn_heads if gdn else (1, LANE)
    cw = _tile(H * dk if gdn else conv_dim, 1024, dk)
    assert col0 % cw == 0 and conv_dim % cw == 0
    tl = _tile(L, 256, SUBLANE)
    c0 = jnp.pad(conv0, ((0, 0), (SUBLANE - (CONV_W - 1), 0), (0, 0)))
    rpb, j0 = tl // SUBLANE, col0 // cw
    if bias is None:
        bias = jnp.zeros((conv_dim,), F32)
    return pl.pallas_call(
        functools.partial(_dwconv_kernel, tl=tl, gdn=gdn, dk=dk, nq=H * dk // cw),
        grid=(B, L // tl, conv_dim // cw),
        in_specs=[pl.BlockSpec((1, tl, cw), lambda b, t, j: (b, t, j0 + j)),
                  pl.BlockSpec((1, SUBLANE, cw), lambda b, t, j: (b, jnp.maximum(t * rpb - 1, 0), j0 + j)),
                  pl.BlockSpec((1, SUBLANE, cw), lambda b, t, j: (b, 0, j)),
                  pl.BlockSpec((CONV_W, cw), lambda b, t, j: (0, j)),
                  pl.BlockSpec((1, cw), lambda b, t, j: (0, j))],
        out_specs=pl.BlockSpec((1, tl, cw), lambda b, t, j: (b, t, j)),
        out_shape=jax.ShapeDtypeStruct((B, L, conv_dim), F32),
        scratch_shapes=[pltpu.VMEM((SUBLANE + tl, cw), F32)],
        compiler_params=_params(("parallel", "parallel", "parallel")),
    )(proj, proj, c0, conv_w, bias.reshape(1, conv_dim))


def gdn_core(qkv, proj, gcs, beta, S0, norm_w, c):
    B, L, _ = qkv.shape
    H, dk, dv = S0.shape[1:]
    hs = 4
    nst = min(H // hs, 4)
    hb = hs * nst
    nc = L // c
    R = hs * c

    def stacked(t):
        return jnp.transpose(t.reshape(B, nc, c, H // hs, hs), (0, 3, 1, 4, 2)).reshape(B, H // hs, nc, R)

    gb = jnp.stack([stacked(gcs), stacked(beta)], axis=3)
    ng = H // hb
    assert (2 * H * dk) % (hb * dv) == 0
    v0 = 2 * H * dk // (hb * dv)
    q_spec = pl.BlockSpec((1, c, hb * dk), lambda b, g, n: (b, n, g))
    k_spec = pl.BlockSpec((1, c, hb * dk), lambda b, g, n: (b, n, ng + g))
    vin_spec = pl.BlockSpec((1, c, hb * dv), lambda b, g, n: (b, n, v0 + g))
    z_spec = pl.BlockSpec((1, c, hb * dv), lambda b, g, n: (b, n, v0 + ng + g))
    y_spec = pl.BlockSpec((1, c, hb * dv), lambda b, g, n: (b, n, g))
    gb_spec = pl.BlockSpec((1, nst, 1, 2, R), lambda b, g, n: (b, g, n, 0, 0))
    s_spec = pl.BlockSpec((1, hb, dk, dv), lambda b, g, n: (b, g, 0, 0))
    return pl.pallas_call(
        functools.partial(_gdn_kernel, nst=nst, hs=hs, c=c, dk=dk, dv=dv),
        grid=(B, ng, nc),
        in_specs=[q_spec, k_spec, vin_spec, z_spec, gb_spec, s_spec, pl.BlockSpec((1, dv), lambda b, g, n: (0, 0))],
        out_specs=[y_spec, s_spec],
        out_shape=[jax.ShapeDtypeStruct((B, L, H * dv), BF16), jax.ShapeDtypeStruct((B, H, dk, dv), F32)],
        compiler_params=_params(("parallel", "parallel", "arbitrary")),
    )(qkv, qkv, qkv, proj, gb, S0, norm_w.reshape(1, dv))


def _ret_kernel(q_ref, k_ref, v_ref, gz_ref, gr_ref, s0_ref, gn_ref, y_ref, s_ref, *, hb, c, dk, dv):
    n = pl.program_id(2)

    @pl.when(n == 0)
    def _():
        s_ref[...] = s0_ref[...]

    row = lax.broadcasted_iota(jnp.int32, (c, c), 0)
    col = lax.broadcasted_iota(jnp.int32, (c, c), 1)
    tri = row >= col
    for h in range(hb):
        qh = q_ref[0, :, h * dk:(h + 1) * dk]
        kh = k_ref[0, :, h * dk:(h + 1) * dk] * (dk ** -0.5)
        vh = v_ref[0, :, h * dv:(h + 1) * dv]
        grw = gr_ref[h, 0]
        gcl = jnp.sum(jnp.where(row == col, grw, 0.0), axis=-1, keepdims=True)
        decay = jnp.where(tri, jnp.exp(jnp.where(tri, gcl - grw, 0.0)), 0.0)
        S = s_ref[0, h]
        s = _bdot_nt(qh, kh) * decay
        o = _bdot(s, vh) + _bdot(qh * jnp.exp(gcl), S)
        g_last = gcl[c - 1:c, :]
        s_ref[0, h] = jnp.exp(g_last) * S + _bdot_tn(kh * jnp.exp(g_last - gcl), vh)
        mu = jnp.mean(o, axis=-1, keepdims=True)
        oc = o - mu
        var = jnp.mean(oc * oc, axis=-1, keepdims=True)
        on = oc * lax.rsqrt(var + NORM_EPS) * gn_ref[h]
        gz = gz_ref[0, :, h * dv:(h + 1) * dv]
        y_ref[0, :, h * dv:(h + 1) * dv] = (_silu(gz) * on).astype(y_ref.dtype)


def ret_core(proj, gcs, S0, gn_w, c):
    B, L, _ = proj.shape
    H, dk, dv = S0.shape[1:]
    hb = 8 if H % 8 == 0 else 4
    nc = L // c
    grow = gcs.T.reshape(H, nc, 1, c)
    ng = H // hb
    assert (2 * H * dk) % (hb * dv) == 0
    v0 = 2 * H * dk // (hb * dv)
    q_spec = pl.BlockSpec((1, c, hb * dk), lambda b, g, n: (b, n, g))
    k_spec = pl.BlockSpec((1, c, hb * dk), lambda b, g, n: (b, n, ng + g))
    vin_spec = pl.BlockSpec((1, c, hb * dv), lambda b, g, n: (b, n, v0 + g))
    gz_spec = pl.BlockSpec((1, c, hb * dv), lambda b, g, n: (b, n, v0 + ng + g))
    v_spec = pl.BlockSpec((1, c, hb * dv), lambda b, g, n: (b, n, g))
    row_spec = pl.BlockSpec((hb, 1, 1, c), lambda b, g, n: (g, n, 0, 0))
    s_spec = pl.BlockSpec((1, hb, dk, dv), lambda b, g, n: (b, g, 0, 0))
    return pl.pallas_call(
        functools.partial(_ret_kernel, hb=hb, c=c, dk=dk, dv=dv),
        grid=(B, H // hb, nc),
        in_specs=[q_spec, k_spec, vin_spec, gz_spec, row_spec, s_spec,
                  pl.BlockSpec((hb, 1, dv), lambda b, g, n: (g, 0, 0))],
        out_specs=[v_spec, s_spec],
        out_shape=[jax.ShapeDtypeStruct((B, L, H * dv), BF16), jax.ShapeDtypeStruct((B, H, dk, dv), F32)],
        compiler_params=_params(("parallel", "parallel", "arbitrary")),
    )(proj, proj, proj, proj, grow, S0, gn_w.reshape(H, 1, dv))


def _lru_kernel(x_ref, gi_ref, wg_ref, bg_ref, sp_ref, h0_ref, y_ref, hl_ref, a_s, b_s, hs_s, h_s, *, tl, seq):
    t = pl.program_id(2)
    x = x_ref[0]
    r = _sigmoid(_bdot(x, wg_ref[0, 0]) + bg_ref[0])
    ig = _sigmoid(_bdot(x, wg_ref[1, 0]) + bg_ref[1])
    log_a = -LRU_C * r * sp_ref[...]
    a = jnp.exp(log_a)
    b = jnp.sqrt(1.0 - jnp.exp(2.0 * log_a)) * (ig * x)
    if seq:
        @pl.when(t == 0)
        def _():
            h_s[...] = h0_ref[0]

        a_s[...] = a
        b_s[...] = b

        def body(i, h):
            h = a_s[pl.ds(i, 1), :] * h + b_s[pl.ds(i, 1), :]
            hs_s[pl.ds(i, 1), :] = h
            return h

        h = lax.fori_loop(0, tl, body, h_s[...], unroll=SUBLANE)
        h_s[...] = h
        hl_ref[0] = h
        hs = hs_s[...]
    else:
        hs = a * h0_ref[0] + b
        hl_ref[0] = hs
    y_ref[0] = (hs * _gelu_tanh(gi_ref[0])).astype(y_ref.dtype)


def lru_core(xc, gate_in, wg2, bg, sp, h0, seq):
    B, L, C = xc.shape
    sb = wg2.shape[-1]
    ns = C // sb
    tl = min(L, 256)
    x_spec = pl.BlockSpec((1, tl, sb), lambda b, s, t: (b, t, s))
    h_rows = 1 if seq else tl
    h_spec = pl.BlockSpec((1, h_rows, sb), (lambda b, s, t: (b, 0, s)) if seq else (lambda b, s, t: (b, t, s)))
    return pl.pallas_call(
        functools.partial(_lru_kernel, tl=tl, seq=seq),
        grid=(B, ns, L // tl),
        in_specs=[x_spec, x_spec,
                  pl.BlockSpec((2, 1, sb, sb), lambda b, s, t: (0, s, 0, 0)),
                  pl.BlockSpec((2, 1, sb), lambda b, s, t: (0, 0, s)),
                  pl.BlockSpec((1, sb), lambda b, s, t: (0, s)),
                  h_spec],
        out_specs=[x_spec, h_spec],
        out_shape=[jax.ShapeDtypeStruct((B, L, C), BF16), jax.ShapeDtypeStruct((B, h_rows if seq else L, C), F32)],
        scratch_shapes=[pltpu.VMEM((tl, sb), F32), pltpu.VMEM((tl, sb), F32), pltpu.VMEM((tl, sb), F32),
                        pltpu.VMEM((1, sb), F32)],
        compiler_params=_params(("parallel", "parallel", "arbitrary")),
    )(xc, gate_in, wg2, bg, sp, h0)


def _moba_prompt_kernel(q_ref, k_ref, v_ref, sl_ref, o_ref, km_s, *, nb, blk, hd, hp):
    qi = pl.program_id(2)

    @pl.when(qi == 0)
    def _():
        km_s[...] = jnp.zeros_like(km_s)
        for n in range(nb):
            km = jnp.mean(k_ref[0, n * blk:(n + 1) * blk, :], axis=0, keepdims=True)
            for j in range(hp):
                km_s[j, n:n + 1, :] = km[:, j * hd:(j + 1) * hd]

    for qv in range(nb):
        @pl.when(qi == qv)
        def _(qv=qv):
            for j in range(hp):
                _moba_query_block(q_ref, k_ref, v_ref, sl_ref, o_ref, km_s, qv, j, blk, hd)


def _moba_query_block(q_ref, k_ref, v_ref, sl_ref, o_ref, km_s, qv, j, blk, hd):
    cols = slice(j * hd, (j + 1) * hd)
    q = q_ref[0, :, cols] * (hd ** -0.5)
    lane = lax.broadcasted_iota(jnp.int32, (blk, LANE), 1)
    sel = jnp.zeros((blk, LANE), F32)
    if qv > 0:
        gate = jnp.where(lane < qv, _bdot_nt(q, km_s[j]), -jnp.inf)
        for _ in range(min(MOBA_TOPK, qv)):
            gmax = jnp.max(gate, axis=-1, keepdims=True)
            imax = jnp.min(jnp.where(gate == gmax, lane, LANE), axis=-1, keepdims=True)
            sel = jnp.where(lane == imax, 1.0, sel)
            gate = jnp.where(lane == imax, -jnp.inf, gate)
    slope = sl_ref[j]
    rowi = lax.broadcasted_iota(jnp.int32, (blk, blk), 0)
    coli = lax.broadcasted_iota(jnp.int32, (blk, blk), 1)
    s_blocks = []
    for n in range(qv + 1):
        s = _bdot_nt(q, k_ref[0, n * blk:(n + 1) * blk, cols])
        dist = (qv - n) * blk + rowi - coli
        ok = (dist >= 0) if n == qv else (sel[:, n:n + 1] > 0.0)
        s_blocks.append(jnp.where(ok, s - slope * dist.astype(F32), NEG_BIG))
    m = s_blocks[0].max(axis=-1, keepdims=True)
    for s in s_blocks[1:]:
        m = jnp.maximum(m, s.max(axis=-1, keepdims=True))
    l = jnp.zeros((blk, 1), F32)
    acc = jnp.zeros((blk, hd), F32)
    for n, s in enumerate(s_blocks):
        p = jnp.where(s > 0.5 * NEG_BIG, jnp.exp(s - m), 0.0)
        l = l + p.sum(axis=-1, keepdims=True)
        acc = acc + _bdot(p, v_ref[0, n * blk:(n + 1) * blk, cols])
    o_ref[0, :, cols] = (acc / l).astype(o_ref.dtype)


def moba_prompt(proj, slopes):
    B, L, D3 = proj.shape
    D = D3 // 3
    H = slopes.shape[0]
    hd = D // H
    blk = MOBA_BLOCK
    nb = L // blk
    hp = 2 if H % 2 == 0 else 1
    ng = H // hp
    q_spec = pl.BlockSpec((1, blk, hp * hd), lambda b, h, i: (b, i, h))
    return pl.pallas_call(
        functools.partial(_moba_prompt_kernel, nb=nb, blk=blk, hd=hd, hp=hp),
        grid=(B, ng, nb),
        in_specs=[q_spec,
                  pl.BlockSpec((1, L, hp * hd), lambda b, h, i: (b, 0, ng + h)),
                  pl.BlockSpec((1, L, hp * hd), lambda b, h, i: (b, 0, 2 * ng + h)),
                  pl.BlockSpec((hp, 1, 1), lambda b, h, i: (h, 0, 0))],
        out_specs=q_spec,
        out_shape=jax.ShapeDtypeStruct((B, L, D), BF16),
        scratch_shapes=[pltpu.VMEM((hp, LANE, hd), F32)],
        compiler_params=_params(("parallel", "parallel", "arbitrary")),
    )(proj, proj, proj, slopes.reshape(H, 1, 1))


def _kblock_sum_kernel(pt_ref, *refs):
    *k_refs, o_ref = refs
    total = jnp.sum(k_refs[0][0, 0], axis=0)
    for k_ref in k_refs[1:]:
        total = total + jnp.sum(k_ref[0, 0], axis=0)
    o_ref[0, 0] = total


def kblock_sums(cache_k, page_table, layer):
    n_seq, n_pages = page_table.shape
    _, _, page, H, hd = cache_k.shape
    ppb = MOBA_BLOCK // page
    page_specs = [pl.BlockSpec((1, 1, page, H, hd), lambda b, n, pt, r=r: (layer, pt[b, n * ppb + r], 0, 0, 0))
                  for r in range(ppb)]
    grid_spec = pltpu.PrefetchScalarGridSpec(
        num_scalar_prefetch=1, grid=(n_seq, n_pages // ppb), in_specs=page_specs,
        out_specs=pl.BlockSpec((1, 1, H, hd), lambda b, n, pt: (b, n, 0, 0)))
    return pl.pallas_call(
        _kblock_sum_kernel, grid_spec=grid_spec,
        out_shape=jax.ShapeDtypeStruct((n_seq, n_pages // ppb, H, hd), F32),
        compiler_params=_params(("parallel", "parallel")),
    )(page_table, *([cache_k] * ppb))


def _moba_step_kernel(pt_ref, q_ref, kn_ref, vn_ref, ks_ref, sl_ref, *refs, page, n_pages, ppb, hd):
    kv_refs, (o_ref, m_s, l_s, acc_s, sel_s) = refs[:2 * ppb], refs[2 * ppb:]
    n = pl.program_id(1)
    q = q_ref[0] * (hd ** -0.5)

    @pl.when(n == 0)
    def _():
        m_s[...] = jnp.sum(q * kn_ref[0], axis=-1, keepdims=True)
        l_s[...] = jnp.ones_like(l_s)
        acc_s[...] = vn_ref[0]
        gate = jnp.sum(ks_ref[0] * (1.0 / MOBA_BLOCK) * q[None], axis=-1, keepdims=True)
        blk = lax.broadcasted_iota(jnp.int32, gate.shape, 0).astype(F32)
        sel = jnp.zeros_like(gate)
        for _ in range(min(MOBA_TOPK, gate.shape[0])):
            gmax = jnp.max(gate, axis=0, keepdims=True)
            imax = jnp.min(jnp.where(gate == gmax, blk, float(2 * gate.shape[0])), axis=0, keepdims=True)
            sel = jnp.where(blk == imax, 1.0, sel)
            gate = jnp.where(blk == imax, -jnp.inf, gate)
        sel_s[...] = sel

    ok = sel_s[n][None] > 0.0
    for r in range(ppb):
        k_ref, v_ref = kv_refs[r], kv_refs[ppb + r]
        s = jnp.sum(k_ref[0, 0] * q[None], axis=-1, keepdims=True)
        tok = lax.broadcasted_iota(jnp.int32, s.shape, 0)
        dist = (n_pages * page - ((n * ppb + r) * page + tok)).astype(F32)
        s = jnp.where(ok, s - sl_ref[...][None] * dist, NEG_BIG)
        m_old = m_s[...]
        m_new = jnp.maximum(m_old, jnp.max(s, axis=0))
        alpha = jnp.exp(m_old - m_new)
        pr = jnp.where(ok, jnp.exp(s - m_new[None]), 0.0)
        l_s[...] = alpha * l_s[...] + jnp.sum(pr, axis=0)
        acc_s[...] = alpha * acc_s[...] + jnp.sum(pr * v_ref[0, 0], axis=0)
        m_s[...] = m_new

    @pl.when(n == n_pages // ppb - 1)
    def _():
        o_ref[0] = (acc_s[...] / l_s[...]).astype(o_ref.dtype)


def moba_step(q, k_new, v_new, k_sums, slopes, cache_k, cache_v, page_table, layer):
    n_seq, n_pages = page_table.shape
    _, _, page, H, hd = cache_k.shape
    n_blk = k_sums.shape[1]
    assert n_blk * MOBA_BLOCK == n_pages * page
    ppb = MOBA_BLOCK // page
    vec = pl.BlockSpec((1, H, hd), lambda b, n, pt: (b, 0, 0))
    pages = [pl.BlockSpec((1, 1, page, H, hd), lambda b, n, pt, r=r: (layer, pt[b, n * ppb + r], 0, 0, 0))
             for r in range(ppb)]
    grid_spec = pltpu.PrefetchScalarGridSpec(
        num_scalar_prefetch=1, grid=(n_seq, n_blk),
        in_specs=[vec, vec, vec,
                  pl.BlockSpec((1, n_blk, H, hd), lambda b, n, pt: (b, 0, 0, 0)),
                  pl.BlockSpec((H, 1), lambda b, n, pt: (0, 0))] + pages + pages,
        out_specs=vec,
        scratch_shapes=[pltpu.VMEM((H, 1), F32), pltpu.VMEM((H, 1), F32), pltpu.VMEM((H, hd), F32),
                        pltpu.VMEM((n_blk, H, 1), F32)])
    return pl.pallas_call(
        functools.partial(_moba_step_kernel, page=page, n_pages=n_pages, ppb=ppb, hd=hd), grid_spec=grid_spec,
        out_shape=jax.ShapeDtypeStruct((n_seq, H, hd), BF16),
        compiler_params=_params(("parallel", "arbitrary")),
    )(page_table, q, k_new, v_new, k_sums, slopes.reshape(H, 1), *([cache_k] * ppb), *([cache_v] * ppb))


def _top_rows(s, pos, k):
    vals, idxs = [], []
    for _ in range(k):
        mx = jnp.max(s, axis=0, keepdims=True)
        ix = jnp.min(jnp.where(s == mx, pos, 1e9), axis=0, keepdims=True)
        s = jnp.where(pos == ix, -jnp.inf, s)
        vals.append(mx)
        idxs.append(ix)
    return vals, idxs


def _peer_route_kernel(q_ref, keys_ref, g_ref, r_ref, c_ref, *, dsub, tt, hp):
    for hh in range(hp):
        _route_head(q_ref, keys_ref, g_ref, r_ref, c_ref, hh, dsub, tt)


def _route_head(q_ref, keys_ref, g_ref, r_ref, c_ref, hh, dsub, tt):
    key_pos = lax.broadcasted_iota(jnp.int32, (N_KEYS, tt), 0).astype(F32)
    s0 = _bdot_nt(keys_ref[2 * hh], q_ref[:, 2 * hh * dsub:(2 * hh + 1) * dsub])
    s1 = _bdot_nt(keys_ref[2 * hh + 1], q_ref[:, (2 * hh + 1) * dsub:(2 * hh + 2) * dsub])
    v0, i0 = _top_rows(s0, key_pos, P_TOPK)
    v1, i1 = _top_rows(s1, key_pos, P_TOPK)
    top_s1 = jnp.concatenate(v1, axis=0)
    top_i0 = jnp.concatenate(i0, axis=0)
    top_i1 = jnp.concatenate(i1, axis=0)
    assert P_TOPK == 2 * SUBLANE
    sub = lax.broadcasted_iota(jnp.int32, (SUBLANE, tt), 0).astype(F32)
    blocks = [v0[0] + top_s1]
    poss = [lax.broadcasted_iota(jnp.int32, (P_TOPK, tt), 0).astype(F32)]
    for a in range(1, SUBLANE):
        blk = v0[a] + top_s1[:SUBLANE]
        blocks.append(jnp.where(sub < float(P_TOPK // (a + 1)), blk, -jnp.inf))
        poss.append(sub + float(a * P_TOPK))
    blocks.append(jnp.concatenate(v0[SUBLANE:], axis=0) + v1[0])
    poss.append((sub + float(SUBLANE)) * float(P_TOPK))
    best_s, best_pos = _top_rows(jnp.concatenate(blocks, axis=0), jnp.concatenate(poss, axis=0), P_TOPK)
    slot = lax.broadcasted_iota(jnp.int32, (P_TOPK, tt), 0).astype(F32)
    e = [jnp.exp(b - best_s[0]) for b in best_s]
    z = e[0]
    for x in e[1:]:
        z = z + x
    for r in range(P_TOPK):
        a = jnp.floor(best_pos[r] * (1.0 / P_TOPK))
        b = best_pos[r] - a * P_TOPK
        o = hh * P_TOPK + r
        g_ref[o:o + 1, :] = e[r] / z
        r_ref[o:o + 1, :] = jnp.sum(jnp.where(slot == a, top_i0, 0.0), axis=0, keepdims=True)
        c_ref[o:o + 1, :] = jnp.sum(jnp.where(slot == b, top_i1, 0.0), axis=0, keepdims=True)


def peer_route(q, keys):
    T = q.shape[0]
    dsub = keys.shape[-1]
    n_heads = keys.shape[0] // 2
    tt = LANE
    hp = 2 if n_heads % 2 == 0 else 1
    out_spec = pl.BlockSpec((hp * P_TOPK, tt), lambda i, h: (h, i))
    shape = jax.ShapeDtypeStruct((n_heads * P_TOPK, T), F32)
    return pl.pallas_call(
        functools.partial(_peer_route_kernel, dsub=dsub, tt=tt, hp=hp),
        grid=(T // tt, n_heads // hp),
        in_specs=[pl.BlockSpec((tt, hp * 2 * dsub), lambda i, h: (i, h)),
                  pl.BlockSpec((hp * 2, N_KEYS, dsub), lambda i, h: (h, 0, 0))],
        out_specs=[out_spec, out_spec, out_spec], out_shape=[shape, shape, shape],
        compiler_params=_params(("parallel", "parallel")),
    )(q, keys)


def _peer_gates_kernel(g_ref, i_ref, j_ref, o_ref, m_s, *, tb):
    iota = lax.broadcasted_iota(jnp.int32, (tb, N_KEYS, LANE), 1).astype(F32)
    a = jnp.where(i_ref[...] == iota, g_ref[...], 0.0)
    a_hi = a.astype(BF16)
    a_lo = (a - a_hi.astype(F32)).astype(BF16)
    bt = jnp.where(j_ref[...] == iota, 1.0, 0.0).astype(BF16)
    dn = (((2,), (2,)), ((0,), (0,)))
    both = lax.dot_general(jnp.concatenate([a_hi, a_lo], axis=1), bt, dn, preferred_element_type=F32)
    m_s[...] = both[:, :N_KEYS] + both[:, N_KEYS:]
    low = [(lax.broadcasted_iota(jnp.int32, (SUBLANE, LANE), 0) & s) == 0 for s in (4, 2, 1)]
    for tg in range(tb // SUBLANE):
        for a in range(N_KEYS // SUBLANE):
            v = [m_s[tg * SUBLANE + r, a * SUBLANE:(a + 1) * SUBLANE, :] for r in range(SUBLANE)]
            for keep, s in zip(low, (4, 2, 1)):
                for k in range(SUBLANE):
                    if k & s == 0:
                        x, y = v[k], v[k + s]
                        v[k] = jnp.where(keep, x, pltpu.roll(y, s, 0))
                        v[k + s] = jnp.where(keep, pltpu.roll(x, SUBLANE - s, 0), y)
            for c in range(SUBLANE):
                i = a * SUBLANE + c
                o_ref[tg * SUBLANE:(tg + 1) * SUBLANE, i * N_KEYS:(i + 1) * N_KEYS] = v[c]


def peer_gate_matrix(gates, rows, cols):
    T, slots = gates.shape
    assert slots == LANE
    tb = min(T, 32)
    spec = pl.BlockSpec((tb, 1, slots), lambda i: (i, 0, 0))
    return pl.pallas_call(
        functools.partial(_peer_gates_kernel, tb=tb),
        grid=(T // tb,), in_specs=[spec, spec, spec],
        out_specs=pl.BlockSpec((tb, N_KEYS * N_KEYS), lambda i: (i, 0)),
        out_shape=jax.ShapeDtypeStruct((T, N_KEYS * N_KEYS), F32),
        scratch_shapes=[pltpu.VMEM((tb, N_KEYS, N_KEYS), F32)],
        compiler_params=_params(("parallel",)),
    )(gates.reshape(T, 1, slots), rows.reshape(T, 1, slots), cols.reshape(T, 1, slots))


def _peer_experts_kernel(x_ref, g_ref, u_ref, v_ref, o_ref):
    e = pl.program_id(1)

    @pl.when(e == 0)
    def _():
        o_ref[...] = jnp.zeros_like(o_ref)

    act = _gelu_tanh(_bdot_nt(x_ref[...], u_ref[...]))
    o_ref[...] += _bdot(g_ref[...] * act, v_ref[...])


def peer_experts(x, gmat, u, v, layer):
    T, D = x.shape
    E = u.shape[1]
    tm = min(T, 512)
    te = 512
    w_spec = pl.BlockSpec((None, te, D), lambda i, e: (layer, e, 0))
    return pl.pallas_call(
        _peer_experts_kernel,
        grid=(T // tm, E // te),
        in_specs=[pl.BlockSpec((tm, D), lambda i, e: (i, 0)), pl.BlockSpec((tm, te), lambda i, e: (i, e)),
                  w_spec, w_spec],
        out_specs=pl.BlockSpec((tm, D), lambda i, e: (i, 0)),
        out_shape=jax.ShapeDtypeStruct((T, D), F32),
        compiler_params=_params(("parallel", "arbitrary")),
    )(x, gmat, u, v)


def peer_ffn(hs, layer, wq, keys, u, v):
    n_heads = keys.shape[0]
    dsub = keys.shape[-1]
    outs = []
    for h, q in zip(hs, proj_pair(hs, wq, layer=layer)):
        B, L, D = h.shape
        T = B * L
        q = jnp.pad(q.reshape(T, -1), ((0, -T % LANE), (0, 0)))
        gates, rows, cols = (t[:, :T].T for t in peer_route(q, keys.reshape(n_heads * 2, N_KEYS, dsub)))
        outs.append(peer_experts(h.reshape(T, D), peer_gate_matrix(gates, rows, cols), u, v, layer).reshape(B, L, D))
    return outs


def _pad_rows(t, rows):
    return jnp.pad(t, ((0, 0), (0, rows - t.shape[1])) + ((0, 0),) * (t.ndim - 2))


def _chunk_cumsum(g, c):
    B, L, H = g.shape
    return jnp.cumsum(g.reshape(B, L // c, c, H), axis=2).reshape(B, L, H)


def gdn_mixer(hs, states, params):
    w_in, w_out = params[0], params[-1]
    H, dk, dv = states[0][0].shape[1:]
    n_main = 2 * H * dk + 2 * H * dv
    projs = proj_pair(hs, w_in, n_cols=n_main)
    bas = proj_pair(hs, w_in[:, n_main:])
    ys, new_states = zip(*(_gdn_group(projs[g], bas[g], states[g], params) for g in range(2)))
    return proj_pair(ys, w_out), new_states


def _gdn_group(proj, ba, state, params):
    S0, conv0 = state
    _, conv_w, a_log, dt_bias, norm_w, _ = params
    B, L, _ = proj.shape
    H, dk, dv = S0.shape[1:]
    conv_dim = 2 * H * dk + H * dv
    conv_new = jnp.concatenate([conv0, proj[:, max(L - (CONV_W - 1), 0):, :conv_dim]], axis=1)[:, -(CONV_W - 1):]
    beta = jax.nn.sigmoid(ba[..., :H])
    g = -jnp.exp(a_log) * jax.nn.softplus(ba[..., H:] + dt_bias)
    c = math.gcd(L, GDN_CHUNK)
    if c < SUBLANE:
        c = SUBLANE
        proj, g, beta = (_pad_rows(t, c) for t in (proj, g, beta))
    qkv = dwconv(proj, 0, conv0, conv_w, gdn_heads=(H, dk))
    y, S = gdn_core(qkv, proj, _chunk_cumsum(g, c), beta, S0, norm_w, c)
    return y[:, :L], (S, conv_new)


def lru_mixer(hs, states, params):
    w_in, w_out = params[0], params[-1]
    projs = proj_pair(hs, w_in)
    ys, new_states = zip(*(_lru_group(projs[g], states[g], params) for g in range(2)))
    return proj_pair(ys, w_out), new_states


def _lru_group(proj, state, params):
    h0, conv0 = state
    _, conv_w, conv_b, w_gates, b_gates, lam, _ = params
    B, L, _ = proj.shape
    C = h0.shape[-1]
    conv_new = jnp.concatenate([conv0, proj[:, max(L - (CONV_W - 1), 0):, C:]], axis=1)[:, -(CONV_W - 1):]
    xc = dwconv(_pad_rows(proj, max(L, SUBLANE)), C, conv0, conv_w, bias=conv_b)[:, :L]
    nb, bw = w_gates.shape[1], w_gates.shape[2]
    per = (bw * LANE // math.gcd(bw, LANE)) // bw
    wg = w_gates.reshape(2, nb // per, per, bw, bw)
    eye = jnp.eye(per, dtype=w_gates.dtype)
    wg2 = jnp.einsum('gspij,pq->gspiqj', wg, eye).reshape(2, nb // per, per * bw, per * bw)
    sp = jax.nn.softplus(-lam).reshape(1, C)
    bg = b_gates.reshape(2, 1, C)
    if L > 1:
        y, hl = lru_core(xc, proj, wg2, bg, sp, h0.reshape(B, 1, C), True)
        h_last = hl.reshape(B, C)
    else:
        y, hl = lru_core(xc.reshape(1, B, C), proj.reshape(1, B, 2 * C), wg2, bg, sp, h0.reshape(1, B, C), False)
        y = y.reshape(B, 1, C)
        h_last = hl.reshape(B, C)
    return y, (h_last, conv_new)


def ret_mixer(hs, states, params):
    w_in, gn_w, w_out = params
    projs = proj_pair(hs, w_in)
    ys, new_states = zip(*(_ret_group(projs[g], states[g], gn_w) for g in range(2)))
    return proj_pair(ys, w_out), new_states


def _ret_group(proj, state, gn_w):
    (S0,) = state
    B, L, _ = proj.shape
    H = S0.shape[1]
    log_gamma = jnp.log1p(-jnp.exp2(-5.0 - jnp.arange(H, dtype=F32)))
    c = math.gcd(L, RET_CHUNK)
    Lp = L
    if c < SUBLANE:
        Lp = c = SUBLANE
        proj = _pad_rows(proj, Lp)
    pos = jnp.arange(Lp)
    steps = jnp.minimum(pos % c + 1, jnp.maximum(L - (pos // c) * c, 0)).astype(F32)
    y, S = ret_core(proj, steps[:, None] * log_gamma[None, :], S0, gn_w, c)
    return y[:, :L], (S,)


def moba_mixer(hs, pasts, params, n_heads):
    w_in, w_out = params
    projs = proj_pair(hs, w_in)
    ys, new_states = zip(*(_moba_group(projs[g], pasts[g], n_heads) for g in range(2)))
    return proj_pair(ys, w_out), new_states


def _moba_group(proj, past, n_heads):
    B, L, D3 = proj.shape
    D = D3 // 3
    hd = D // n_heads
    q, k, v = proj[..., :D], proj[..., D:2 * D], proj[..., 2 * D:]
    slopes = jnp.exp2(-8.0 * (jnp.arange(n_heads, dtype=F32) + 1.0) / n_heads)
    if past is None:
        o = moba_prompt(proj, slopes)
    else:
        cache_k, cache_v, page_table, layer = past
        k_sums = kblock_sums(cache_k, page_table, layer)
        o = moba_step(q.reshape(B, n_heads, hd), k.reshape(B, n_heads, hd), v.reshape(B, n_heads, hd), k_sums,
                      slopes, cache_k, cache_v, page_table, layer).reshape(B, L, D)
    return o, (k.reshape(B, L, n_heads, hd), v.reshape(B, L, n_heads, hd))


def kernel(x_prompt, x_sample, c_prompt, c_sample, state_gdn_S, state_gdn_conv, state_lru_h, state_lru_conv, state_ret_S, cache_k, cache_v, page_table, w_ada, b_ada, ln_g, ln_b, peer_wq, peer_keys, peer_u, peer_v, gdn_w_in, gdn_conv_w, gdn_a_log, gdn_dt_bias, gdn_norm_w, gdn_w_out, lru_w_in, lru_conv_w, lru_conv_b, lru_w_gates, lru_b_gates, lru_lambda, lru_w_out, ret_w_in, ret_gn_w, ret_w_out, moba_w_in, moba_w_out):
    depth, D = w_ada.shape[0], w_ada.shape[1]
    n_mixers = 4
    Bp, Bs = x_prompt.shape[0], x_sample.shape[0]
    H_A, DK_A, DV_A = state_gdn_S.shape[2:]
    conv_dim_a = state_gdn_conv.shape[-1]
    C_B = state_lru_h.shape[-1]
    H_C, DK_C, DV_C = state_ret_S.shape[2:]
    H_D = cache_k.shape[3]

    c_all = jax.nn.silu(jnp.concatenate([c_prompt, c_sample], axis=0))
    n_c = c_all.shape[0]
    c_pad = jnp.pad(c_all, ((0, -n_c % SUBLANE), (0, 0)))
    xs = [x_prompt, x_sample]
    groups = ((0, Bp), (Bp, Bs))
    new_states = {(m, grp): [] for m in range(n_mixers) for grp in range(2)}
    hs = [None, None]
    mod_all = [matmul(c_pad, w_ada, i, bias=b_ada[i]) for i in range(depth)]
    u_bf, v_bf = peer_u.astype(BF16), peer_v.astype(BF16)
    mods = [[[mod[s:s + n, None, r * D:(r + 1) * D] for r in range(6)] for s, n in groups] for mod in mod_all]
    for i in range(depth):
        m, j = i % n_mixers, i // n_mixers
        if i == 0:
            hs = [modulate(xs[g], mods[0][g][1], mods[0][g][0]) for g in range(2)]
        if m == 0:
            params = (gdn_w_in[j], gdn_conv_w[j], gdn_a_log[j], gdn_dt_bias[j], gdn_norm_w[j], gdn_w_out[j])
            states = ((jnp.zeros((Bp, H_A, DK_A, DV_A), F32), jnp.zeros((Bp, CONV_W - 1, conv_dim_a), F32)),
                      (state_gdn_S[j], state_gdn_conv[j]))
            ys, ns = gdn_mixer(hs, states, params)
        elif m == 1:
            params = (lru_w_in[j], lru_conv_w[j], lru_conv_b[j], lru_w_gates[j], lru_b_gates[j], lru_lambda[j],
                      lru_w_out[j])
            states = ((jnp.zeros((Bp, C_B), F32), jnp.zeros((Bp, CONV_W - 1, C_B), F32)),
                      (state_lru_h[j], state_lru_conv[j]))
            ys, ns = lru_mixer(hs, states, params)
        elif m == 2:
            states = ((jnp.zeros((Bp, H_C, DK_C, DV_C), F32),), (state_ret_S[j],))
            ys, ns = ret_mixer(hs, states, (ret_w_in[j], ret_gn_w[j], ret_w_out[j]))
        else:
            ys, ns = moba_mixer(hs, (None, (cache_k, cache_v, page_table, j)), (moba_w_in[j], moba_w_out[j]), H_D)
        x1, h2 = [], []
        for g in range(2):
            new_states[(m, g)].append(ns[g])
            _, _, g1, sh2, sc2, _ = mods[i][g]
            xg, hg = ln_residual(xs[g], ys[g], g1, sc2, sh2, ln_g[i, 0], ln_b[i, 0])
            x1.append(xg)
            h2.append(hg)
        y2 = peer_ffn(h2, i, peer_wq, peer_keys[i], u_bf, v_bf)
        for g in range(2):
            if i + 1 < depth:
                sh_n, sc_n = mods[i + 1][g][0], mods[i + 1][g][1]
            else:
                sh_n = sc_n = jnp.zeros((xs[g].shape[0], 1, D), F32)
            xs[g], hs[g] = ln_residual(x1[g], y2[g], mods[i][g][5], sc_n, sh_n, ln_g[i, 1], ln_b[i, 1])

    def stacked(m, grp):
        return [jnp.stack(parts) for parts in zip(*new_states[(m, grp)])]

    gdn_S_p, gdn_conv_p = stacked(0, 0)
    gdn_S_s, gdn_conv_s = stacked(0, 1)
    lru_h_p, lru_conv_p = stacked(1, 0)
    lru_h_s, lru_conv_s = stacked(1, 1)
    (ret_S_p,) = stacked(2, 0)
    (ret_S_s,) = stacked(2, 1)
    k_p, v_p = stacked(3, 0)
    k_s, v_s = stacked(3, 1)
    return (xs[0], xs[1], gdn_S_p, gdn_S_s, gdn_conv_p, gdn_conv_s, lru_h_p, lru_h_s, lru_conv_p, lru_conv_s,
            ret_S_p, ret_S_s, k_p, k_s, v_p, v_s)
```
